```python
import math
import jax, jax.numpy as jnp
from jax import lax
import numpy as np

D_MODEL = 1024
BATCH = 16
SEQ = 2048
DEPTH = 4

N_EVEN = (DEPTH + 1) // 2
N_ODD = DEPTH // 2
ROPE_THETA = 10000.0
BLOCK = 128
NORM_EPS = 1e-5

RET_HEADS = 8
RET_DK = D_MODEL // RET_HEADS
RET_DV = D_MODEL // RET_HEADS
RET_QK_WIDTH = RET_HEADS * RET_DK
RET_WIDTH = RET_HEADS * RET_DV

LRU_WIDTH = D_MODEL
LRU_BLOCKS = 8
LRU_BLOCK_DIM = LRU_WIDTH // LRU_BLOCKS
CONV_WIDTH = 4
LRU_C = 8.0

DIFF_HEADS = 8
DIFF_DH = D_MODEL // DIFF_HEADS
DIFF_DV = 2 * DIFF_DH
DIFF_QK_WIDTH = 2 * DIFF_HEADS * DIFF_DH
DIFF_WIDTH = DIFF_HEADS * DIFF_DV

EVEN_IN = 3 * RET_QK_WIDTH + RET_WIDTH + 2 * LRU_WIDTH
EVEN_MIX = RET_WIDTH + LRU_WIDTH
ODD_IN = 2 * DIFF_QK_WIDTH + 2 * DIFF_WIDTH
ODD_MIX = DIFF_WIDTH

DEEPNORM_ALPHA = (2.0 * DEPTH) ** 0.25
DEEPNORM_BETA = (8.0 * DEPTH) ** -0.25

kernel_name = "hybrid_retention_rglru_diffattn_deepnorm"


def rope(x, pos):
    half = x.shape[-1] // 2
    inv_freq = ROPE_THETA ** (-jnp.arange(half, dtype=jnp.float32) / half)
    ang = pos.astype(jnp.float32)[:, None] * inv_freq[None, :]
    cos = jnp.cos(ang)[:, None, :]
    sin = jnp.sin(ang)[:, None, :]
    x1 = x[..., :half].astype(jnp.float32)
    x2 = x[..., half:].astype(jnp.float32)
    return jnp.concatenate([x1 * cos - x2 * sin, x2 * cos + x1 * sin], axis=-1)


def layer_norm(x, g, b):
    x = x.astype(jnp.float32)
    mu = jnp.mean(x, axis=-1, keepdims=True)
    var = jnp.mean(jnp.square(x - mu), axis=-1, keepdims=True)
    return (x - mu) * lax.rsqrt(var + NORM_EPS) * g + b


def retention(q, k, v):
    B, S, H, dk = q.shape
    dv = v.shape[-1]
    N = S // BLOCK
    log_g = jnp.log(1.0 - 2.0 ** (-5.0 - jnp.arange(H, dtype=jnp.float32)))
    idx = jnp.arange(BLOCK, dtype=jnp.float32)
    rel = idx[:, None] - idx[None, :]
    decay = jnp.where(rel[None] >= 0, jnp.exp(log_g[:, None, None] * jnp.maximum(rel, 0.0)[None]), 0.0)
    qc = q.reshape(B, N, BLOCK, H, dk)
    kc = k.reshape(B, N, BLOCK, H, dk)
    vc = v.astype(jnp.float32).reshape(B, N, BLOCK, H, dv)
    scores = jnp.einsum('bnihd,bnjhd->bnhij', qc, kc) * decay
    intra = jnp.einsum('bnhij,bnjhe->bnihe', scores, vc)
    k_end = kc * jnp.exp(log_g[None, :] * (BLOCK - 1 - idx)[:, None])[:, :, None]
    kv = jnp.einsum('bnjhd,bnjhe->nbhde', k_end, vc)
    chunk_decay = jnp.exp(log_g * BLOCK)[None, :, None, None]

    def step(R, kv_n):
        return R * chunk_decay + kv_n, R

    _, R_prev = lax.scan(step, jnp.zeros((B, H, dk, dv), jnp.float32), kv)
    q_dec = qc * jnp.exp(log_g[None, :] * (idx + 1.0)[:, None])[:, :, None]
    cross = jnp.einsum('bnihd,nbhde->bnihe', q_dec, R_prev)
    return (intra + cross).reshape(B, S, H, dv)


def rg_lru_branch(xb, conv_w, conv_b, wa, ba, wx, bx, lam):
    B, S, W = xb.shape
    xb = xb.astype(jnp.float32)
    xc = lax.conv_general_dilated(
        xb, conv_w.astype(jnp.float32)[:, None, :], window_strides=(1,),
        padding=[(CONV_WIDTH - 1, 0)], dimension_numbers=('NWC', 'WIO', 'NWC'),
        feature_group_count=W) + conv_b
    xg = xc.reshape(B, S, LRU_BLOCKS, LRU_BLOCK_DIM)
    r = jax.nn.sigmoid(jnp.einsum('bsgi,gij->bsgj', xg, wa).reshape(B, S, W) + ba)
    i = jax.nn.sigmoid(jnp.einsum('bsgi,gij->bsgj', xg, wx).reshape(B, S, W) + bx)
    log_a = -LRU_C * r * jax.nn.softplus(-lam.astype(jnp.float32))
    a = jnp.exp(log_a)
    u = xc * i * jnp.sqrt(-jnp.expm1(2.0 * log_a))

    def combine(e1, e2):
        a1, b1 = e1
        a2, b2 = e2
        return a1 * a2, a2 * b1 + b2

    _, h = lax.associative_scan(combine, (a, u), axis=1)
    return h


def diff_attention(q, k, v, lam):
    B, S, H2, d = q.shape
    H = H2 // 2
    nb = S // BLOCK
    v = v.astype(jnp.float32)
    qb = q.reshape(B, nb, BLOCK, H2, d).transpose(1, 0, 2, 3, 4)
    kpos = jnp.arange(S)

    def one_block(args):
        qblk, start = args
        s = jnp.einsum('bqhd,bkhd->bhqk', qblk, k)
        qpos = start + jnp.arange(BLOCK)
        s = jnp.where((kpos[None, :] <= qpos[:, None])[None, None], s, -jnp.inf)
        p = jax.nn.softmax(s, axis=-1).reshape(B, H, 2, BLOCK, S)
        w = p[:, :, 0] - lam * p[:, :, 1]
        return jnp.einsum('bhqk,bkhe->bqhe', w, v)

    out = lax.map(one_block, (qb, jnp.arange(nb) * BLOCK))
    return out.transpose(1, 0, 2, 3, 4).reshape(B, S, H, 2 * d)


def even_layer(x, w_in, conv_w, conv_b, wa, ba, wx, bx, lam, gn_g, gn_b, w_out, pos):
    B, S, _ = x.shape
    proj = jnp.einsum('bsd,de->bse', x, w_in)
    q, k, v, g_ret, xb, g_lru = jnp.split(
        proj, [RET_QK_WIDTH, 2 * RET_QK_WIDTH, 2 * RET_QK_WIDTH + RET_WIDTH,
               2 * RET_QK_WIDTH + 2 * RET_WIDTH, 2 * RET_QK_WIDTH + 2 * RET_WIDTH + LRU_WIDTH], axis=-1)
    q = rope(q.reshape(B, S, RET_HEADS, RET_DK), pos)
    k = rope(k.reshape(B, S, RET_HEADS, RET_DK), pos) * (RET_DK ** -0.5)
    ret = retention(q, k, v.reshape(B, S, RET_HEADS, RET_DV))
    mu = jnp.mean(ret, axis=-1, keepdims=True)
    var = jnp.mean(jnp.square(ret - mu), axis=-1, keepdims=True)
    ret = ((ret - mu) * lax.rsqrt(var + NORM_EPS)).reshape(B, S, RET_WIDTH) * gn_g + gn_b
    ret = ret * jax.nn.silu(g_ret.astype(jnp.float32))
    lru = rg_lru_branch(xb, conv_w, conv_b, wa, ba, wx, bx, lam) * jax.nn.silu(g_lru.astype(jnp.float32))
    return jnp.einsum('bse,ed->bsd', jnp.concatenate([ret, lru], axis=-1), w_out)


def odd_layer(x, w_in, lq1, lk1, lq2, lk2, subln_g, w_out, pos, lambda_init):
    B, S, _ = x.shape
    proj = jnp.einsum('bsd,de->bse', x, w_in)
    q, k, v, g = jnp.split(proj, [DIFF_QK_WIDTH, 2 * DIFF_QK_WIDTH, 2 * DIFF_QK_WIDTH + DIFF_WIDTH], axis=-1)
    q = rope(q.reshape(B, S, 2 * DIFF_HEADS, DIFF_DH), pos) * (DIFF_DH ** -0.5)
    k = rope(k.reshape(B, S, 2 * DIFF_HEADS, DIFF_DH), pos)
    lam = (jnp.exp(jnp.sum(lq1.astype(jnp.float32) * lk1)) - jnp.exp(jnp.sum(lq2.astype(jnp.float32) * lk2))
           + lambda_init)
    out = diff_attention(q, k, v.reshape(B, S, DIFF_HEADS, DIFF_DV), lam)
    out = out * lax.rsqrt(jnp.mean(jnp.square(out), axis=-1, keepdims=True) + NORM_EPS) * subln_g
    out = (out * (1.0 - lambda_init)).reshape(B, S, DIFF_WIDTH) * jax.nn.silu(g.astype(jnp.float32))
    return jnp.einsum('bse,ed->bsd', out, w_out)


def setup_inputs(seed: int = 0) -> dict:
    key = jax.random.key(seed)
    ks = jax.random.split(key, 32)
    f32 = jnp.float32
    nrm = lambda k, shape, s: jax.random.normal(k, shape, f32) * s
    ev_scale = jnp.concatenate([jnp.ones((2 * RET_QK_WIDTH,), f32), jnp.full((RET_WIDTH,), DEEPNORM_BETA, f32),
                                jnp.ones((RET_WIDTH,), f32), jnp.full((LRU_WIDTH,), DEEPNORM_BETA, f32),
                                jnp.ones((LRU_WIDTH,), f32)])
    od_scale = jnp.concatenate([jnp.ones((2 * DIFF_QK_WIDTH,), f32), jnp.full((DIFF_WIDTH,), DEEPNORM_BETA, f32),
                                jnp.ones((DIFF_WIDTH,), f32)])
    u = jax.random.uniform(ks[8], (N_EVEN, LRU_WIDTH), f32, 0.9, 0.999)
    a0 = u ** (1.0 / LRU_C)
    return {
        "x": nrm(ks[0], (BATCH, SEQ, D_MODEL), 1.0),
        "ev_w_in": nrm(ks[1], (N_EVEN, D_MODEL, EVEN_IN), D_MODEL ** -0.5) * ev_scale,
        "ev_conv_w": nrm(ks[2], (N_EVEN, CONV_WIDTH, LRU_WIDTH), CONV_WIDTH ** -0.5),
        "ev_conv_b": nrm(ks[3], (N_EVEN, LRU_WIDTH), 0.01),
        "ev_gate_a_w": nrm(ks[4], (N_EVEN, LRU_BLOCKS, LRU_BLOCK_DIM, LRU_BLOCK_DIM), LRU_BLOCK_DIM ** -0.5),
        "ev_gate_a_b": nrm(ks[5], (N_EVEN, LRU_WIDTH), 0.01),
        "ev_gate_x_w": nrm(ks[6], (N_EVEN, LRU_BLOCKS, LRU_BLOCK_DIM, LRU_BLOCK_DIM), LRU_BLOCK_DIM ** -0.5),
        "ev_gate_x_b": nrm(ks[7], (N_EVEN, LRU_WIDTH), 0.01),
        "ev_lru_lambda": jnp.log(a0) - jnp.log1p(-a0),
        "ev_ret_gn_g": 1.0 + nrm(ks[9], (N_EVEN, RET_WIDTH), 0.02),
        "ev_ret_gn_b": nrm(ks[10], (N_EVEN, RET_WIDTH), 0.02),
        "ev_w_out": nrm(ks[11], (N_EVEN, EVEN_MIX, D_MODEL), EVEN_MIX ** -0.5 * DEEPNORM_BETA),
        "ev_ln_g": 1.0 + nrm(ks[12], (N_EVEN, D_MODEL), 0.02),
        "ev_ln_b": nrm(ks[13], (N_EVEN, D_MODEL), 0.02),
        "od_w_in": nrm(ks[14], (N_ODD, D_MODEL, ODD_IN), D_MODEL ** -0.5) * od_scale,
        "od_lambda_q1": nrm(ks[15], (N_ODD, DIFF_DH), 0.1),
        "od_lambda_k1": nrm(ks[16], (N_ODD, DIFF_DH), 0.1),
        "od_lambda_q2": nrm(ks[17], (N_ODD, DIFF_DH), 0.1),
        "od_lambda_k2": nrm(ks[18], (N_ODD, DIFF_DH), 0.1),
        "od_subln_g": 1.0 + nrm(ks[19], (N_ODD, DIFF_DV), 0.02),
        "od_w_out": nrm(ks[20], (N_ODD, ODD_MIX, D_MODEL), ODD_MIX ** -0.5 * DEEPNORM_BETA),
        "od_ln_g": 1.0 + nrm(ks[21], (N_ODD, D_MODEL), 0.02),
        "od_ln_b": nrm(ks[22], (N_ODD, D_MODEL), 0.02),
    }


def reference(x, ev_w_in, ev_conv_w, ev_conv_b, ev_gate_a_w, ev_gate_a_b, ev_gate_x_w, ev_gate_x_b,
              ev_lru_lambda, ev_ret_gn_g, ev_ret_gn_b, ev_w_out, ev_ln_g, ev_ln_b,
              od_w_in, od_lambda_q1, od_lambda_k1, od_lambda_q2, od_lambda_k2, od_subln_g, od_w_out,
              od_ln_g, od_ln_b):
    in_dtype = x.dtype
    pos = jnp.arange(x.shape[1], dtype=jnp.int32)
    h = x
    for l in range(DEPTH):
        j = l // 2
        if l % 2 == 0:
            y = even_layer(h, ev_w_in[j], ev_conv_w[j], ev_conv_b[j], ev_gate_a_w[j], ev_gate_a_b[j],
                           ev_gate_x_w[j], ev_gate_x_b[j], ev_lru_lambda[j], ev_ret_gn_g[j], ev_ret_gn_b[j],
                           ev_w_out[j], pos)
            h = layer_norm(DEEPNORM_ALPHA * h.astype(jnp.float32) + y, ev_ln_g[j], ev_ln_b[j])
        else:
            lambda_init = 0.8 - 0.6 * math.exp(-0.3 * l)
            y = odd_layer(h, od_w_in[j], od_lambda_q1[j], od_lambda_k1[j], od_lambda_q2[j], od_lambda_k2[j],
                          od_subln_g[j], od_w_out[j], pos, lambda_init)
            h = layer_norm(DEEPNORM_ALPHA * h.astype(jnp.float32) + y, od_ln_g[j], od_ln_b[j])
    return h.astype(in_dtype)
```

```python
import functools
import math

import jax
import jax.numpy as jnp
from jax import lax
from jax.experimental import pallas as pl
from jax.experimental.pallas import tpu as pltpu

F32 = jnp.float32
BF16 = jnp.bfloat16

D_MODEL = 1024
DEPTH = 4
ROPE_THETA = 10000.0
CHUNK = 128
NORM_EPS = 1e-5
HEAD = 128

RET_HEADS = 8
LRU_WIDTH = 1024
LRU_BLOCKS = 8
CONV_WIDTH = 4
LRU_C = 8.0
DIFF_HEADS = 8
DIFF_DV = 2 * HEAD

DEEPNORM_ALPHA = (2.0 * DEPTH) ** 0.25

VMEM_LIMIT = 48 * 1024 * 1024

PROJ_TM = 512
PROJ_TN = 1024
OUT_TM = 512
LRU_T = 256
ATT_TQ = 256
ATT_TK = 256


def _params(sem):
    return pltpu.CompilerParams(dimension_semantics=sem, vmem_limit_bytes=VMEM_LIMIT)


def _proj_kernel(x_ref, w_ref, cos_ref, sin_ref, o_ref, *, n_rope_blocks, tn):
    j = pl.program_id(0)
    acc = jnp.dot(x_ref[...], w_ref[...], preferred_element_type=F32)

    @pl.when(j < n_rope_blocks)
    def _():
        cos = cos_ref[0]
        sin = sin_ref[0]
        for h in range(tn // HEAD):
            a = acc[:, h * HEAD:(h + 1) * HEAD]
            r = a * cos + pltpu.roll(a, HEAD // 2, 1) * sin
            o_ref[:, h * HEAD:(h + 1) * HEAD] = r.astype(o_ref.dtype)

    @pl.when(j >= n_rope_blocks)
    def _():
        o_ref[...] = acc.astype(o_ref.dtype)


def _project(xb, w, cos_tab, sin_tab, rope_cols, seq):
    m, k = xb.shape
    n = w.shape[1]
    tm, tn = PROJ_TM, PROJ_TN
    n_rope_blocks = rope_cols // tn
    half_blocks = n_rope_blocks // 2
    seq_blocks = seq // tm

    def tab_map(j, i):
        return (jnp.minimum(j // half_blocks, 1), i % seq_blocks, 0)

    return pl.pallas_call(
        functools.partial(_proj_kernel, n_rope_blocks=n_rope_blocks, tn=tn),
        grid=(n // tn, m // tm),
        in_specs=[
            pl.BlockSpec((tm, k), lambda j, i: (i, 0)),
            pl.BlockSpec((k, tn), lambda j, i: (0, j)),
            pl.BlockSpec((1, tm, HEAD), tab_map),
            pl.BlockSpec((1, tm, HEAD), tab_map),
        ],
        out_specs=pl.BlockSpec((tm, tn), lambda j, i: (i, j)),
        out_shape=jax.ShapeDtypeStruct((m, n), BF16),
        compiler_params=_params(("arbitrary", "arbitrary")),
        name="proj",
    )(xb, w, cos_tab, sin_tab)


def _retention_kernel(cd_ref, q_ref, k_ref, v_ref, g_ref, decay_ref, qdec_ref, kend_ref,
                      gng_ref, gnb_ref, o_ref, state_ref):
    n = pl.program_id(1)

    @pl.when(n == 0)
    def _():
        state_ref[...] = jnp.zeros_like(state_ref)

    for h in range(RET_HEADS):
        hs = slice(h * HEAD, (h + 1) * HEAD)
        q = q_ref[:, hs]
        k = k_ref[:, hs]
        v = v_ref[:, hs]
        s = lax.dot_general(q, k, (((1,), (1,)), ((), ())), preferred_element_type=F32)
        s = s * decay_ref[h]
        intra = jnp.dot(s.astype(BF16), v, preferred_element_type=F32)
        state = state_ref[h]
        cross = jnp.dot(q, state.astype(BF16), preferred_element_type=F32) * qdec_ref[h]
        v_end = (v.astype(F32) * kend_ref[h]).astype(BF16)
        kv = lax.dot_general(k, v_end, (((0,), (0,)), ((), ())), preferred_element_type=F32)
        state_ref[h] = state * cd_ref[h] + kv
        o = intra + cross
        mu = jnp.mean(o, axis=-1, keepdims=True)
        d = o - mu
        var = jnp.mean(d * d, axis=-1, keepdims=True)
        o = d * lax.rsqrt(var + NORM_EPS) * gng_ref[:, hs] + gnb_ref[:, hs]
        g = g_ref[:, hs].astype(F32)
        o = o * (g * jax.nn.sigmoid(g))
        o_ref[:, hs] = o.astype(o_ref.dtype)


def _retention(proj, gn_g, gn_b, batch, seq):
    m = proj.shape[0]
    nchunk = seq // CHUNK
    width = RET_HEADS * HEAD
    log_g = jnp.log(1.0 - 2.0 ** (-5.0 - jnp.arange(RET_HEADS, dtype=F32)))
    idx = jnp.arange(CHUNK, dtype=F32)
    rel = idx[:, None] - idx[None, :]
    decay = jnp.where(rel[None] >= 0,
                      jnp.exp(log_g[:, None, None] * jnp.maximum(rel, 0.0)[None]), 0.0)
    kend = jnp.exp(log_g[:, None] * (CHUNK - 1 - idx)[None, :])
    qdec = jnp.exp(log_g[:, None] * (idx + 1.0)[None, :])
    kend = jnp.broadcast_to(kend[:, :, None], (RET_HEADS, CHUNK, HEAD))
    qdec = jnp.broadcast_to(qdec[:, :, None], (RET_HEADS, CHUNK, HEAD))
    chunk_decay = jnp.exp(log_g * CHUNK)

    def row(b, n):
        return b * nchunk + n

    def col_spec(c):
        return pl.BlockSpec((CHUNK, width), lambda b, n, c=c: (row(b, n), c))

    table = pl.BlockSpec((RET_HEADS, CHUNK, HEAD), lambda b, n: (0, 0, 0))
    vec = pl.BlockSpec((1, width), lambda b, n: (0, 0))
    return pl.pallas_call(
        _retention_kernel,
        grid=(batch, nchunk),
        in_specs=[
            pl.BlockSpec(memory_space=pltpu.SMEM),
            col_spec(0), col_spec(1), col_spec(2), col_spec(3),
            table, table, table, vec, vec,
        ],
        out_specs=pl.BlockSpec((CHUNK, width), lambda b, n: (row(b, n), 0)),
        out_shape=jax.ShapeDtypeStruct((m, width), BF16),
        scratch_shapes=[pltpu.VMEM((RET_HEADS, HEAD, HEAD), F32)],
        compiler_params=_params(("arbitrary", "arbitrary")),
        name="retention",
    )(chunk_decay, proj, proj, proj, proj, decay, qdec, kend,
      gn_g.reshape(1, width), gn_b.reshape(1, width))


def _softplus(z):
    return jnp.maximum(z, 0.0) + jnp.log1p(jnp.exp(-jnp.abs(z)))


def _lru_kernel(x_ref, g_ref, cw_ref, cb_ref, wa_ref, ba_ref, wx_ref, bx_ref, lam_ref,
                o_ref, xpad_ref, a_ref, u_ref, h_ref, carry_ref, *, t_blk):
    t = pl.program_id(1)
    pad = 8

    @pl.when(t == 0)
    def _():
        xpad_ref[0:pad, :] = jnp.zeros((pad, LRU_WIDTH), F32)
        carry_ref[...] = jnp.zeros_like(carry_ref)

    xpad_ref[pad:pad + t_blk, :] = x_ref[...].astype(F32)
    xc = cb_ref[...] + jnp.zeros((t_blk, LRU_WIDTH), F32)
    for w in range(CONV_WIDTH):
        off = pad - (CONV_WIDTH - 1) + w
        xc = xc + cw_ref[w:w + 1, :] * xpad_ref[off:off + t_blk, :]
    xpad_ref[0:pad, :] = xpad_ref[t_blk:t_blk + pad, :]

    neg_c_sp = -LRU_C * _softplus(-lam_ref[...])
    bd = LRU_WIDTH // LRU_BLOCKS
    for blk in range(LRU_BLOCKS):
        cs = slice(blk * bd, (blk + 1) * bd)
        xg = xc[:, cs]
        xgb = xg.astype(BF16)
        r = jax.nn.sigmoid(jnp.dot(xgb, wa_ref[blk], preferred_element_type=F32) + ba_ref[:, cs])
        i = jax.nn.sigmoid(jnp.dot(xgb, wx_ref[blk], preferred_element_type=F32) + bx_ref[:, cs])
        log_a = r * neg_c_sp[:, cs]
        a = jnp.exp(log_a)
        a_ref[:, cs] = a
        u_ref[:, cs] = xg * i * jnp.sqrt(-jnp.tanh(log_a) * (a * a + 1.0))

    def body(i, h):
        r0 = pl.multiple_of(i * 8, 8)
        a8 = a_ref[pl.ds(r0, 8), :]
        u8 = u_ref[pl.ds(r0, 8), :]
        rows = []
        for j in range(8):
            h = a8[j:j + 1, :] * h + u8[j:j + 1, :]
            rows.append(h)
        h_ref[pl.ds(r0, 8), :] = jnp.concatenate(rows, axis=0)
        return h

    carry_ref[...] = lax.fori_loop(0, t_blk // 8, body, carry_ref[...])
    g = g_ref[...].astype(F32)
    o_ref[...] = (h_ref[...] * (g * jax.nn.sigmoid(g))).astype(o_ref.dtype)


def _rg_lru(proj, conv_w, conv_b, wa, ba, wx, bx, lam, batch, seq):
    m = proj.shape[0]
    t_blk = LRU_T
    nblk = seq // t_blk
    w = LRU_WIDTH
    bd = w // LRU_BLOCKS

    def col_spec(c):
        return pl.BlockSpec((t_blk, w), lambda b, t, c=c: (b * nblk + t, c))

    vec = pl.BlockSpec((1, w), lambda b, t: (0, 0))
    gate_w = pl.BlockSpec((LRU_BLOCKS, bd, bd), lambda b, t: (0, 0, 0))
    return pl.pallas_call(
        functools.partial(_lru_kernel, t_blk=t_blk),
        grid=(batch, nblk),
        in_specs=[
            col_spec(4), col_spec(5),
            pl.BlockSpec((CONV_WIDTH, w), lambda b, t: (0, 0)), vec,
            gate_w, vec, gate_w, vec, vec,
        ],
        out_specs=pl.BlockSpec((t_blk, w), lambda b, t: (b * nblk + t, 0)),
        out_shape=jax.ShapeDtypeStruct((m, w), BF16),
        scratch_shapes=[
            pltpu.VMEM((t_blk + 8, w), F32),
            pltpu.VMEM((t_blk, w), F32),
            pltpu.VMEM((t_blk, w), F32),
            pltpu.VMEM((t_blk, w), F32),
            pltpu.VMEM((1, w), F32),
        ],
        compiler_params=_params(("arbitrary", "arbitrary")),
        name="rg_lru",
    )(proj, proj, conv_w, conv_b.reshape(1, w), wa.astype(BF16), ba.reshape(1, w),
      wx.astype(BF16), bx.reshape(1, w), lam.reshape(1, w))


def _attn_kernel(q_ref, k_ref, v_ref, g_ref, lq1_ref, lk1_ref, lq2_ref, lk2_ref, sg_ref,
                 o_ref, m_ref, l_ref, acc_ref, *, tq, tk, lambda_init):
    qi = pl.program_id(2)
    m_ref[...] = jnp.full_like(m_ref, -jnp.inf)
    l_ref[...] = jnp.zeros_like(l_ref)
    acc_ref[...] = jnp.zeros_like(acc_ref)

    def step(kb, masked):
        k0 = pl.multiple_of(kb * tk, tk)
        k = k_ref[pl.ds(k0, tk), :]
        v = v_ref[pl.ds(k0, tk), :]
        for c in range(2):
            cs = slice(c * HEAD, (c + 1) * HEAD)
            s = lax.dot_general(q_ref[:, cs], k[:, cs], (((1,), (1,)), ((), ())),
                                preferred_element_type=F32)
            if masked:
                rows = lax.broadcasted_iota(jnp.int32, (tq, tk), 0)
                cols = lax.broadcasted_iota(jnp.int32, (tq, tk), 1)
                s = jnp.where(cols <= rows, s, -jnp.inf)
            m_old = m_ref[c]
            m_new = jnp.maximum(m_old, jnp.max(s, axis=-1, keepdims=True))
            alpha = jnp.exp(m_old - m_new)
            p = jnp.exp(s - m_new)
            l_ref[c] = alpha * l_ref[c] + jnp.sum(p, axis=-1, keepdims=True)
            acc_ref[c] = alpha * acc_ref[c] + jnp.dot(p.astype(BF16), v,
                                                      preferred_element_type=F32)
            m_ref[c] = m_new

    def body(kb, carry):
        step(kb, False)
        return carry

    lax.fori_loop(0, qi, body, 0)
    step(qi, True)

    lam = (jnp.exp(jnp.sum(lq1_ref[...] * lk1_ref[...], axis=-1, keepdims=True))
           - jnp.exp(jnp.sum(lq2_ref[...] * lk2_ref[...], axis=-1, keepdims=True))
           + lambda_init)
    o = acc_ref[0] / l_ref[0] - lam * (acc_ref[1] / l_ref[1])
    ms = jnp.mean(o * o, axis=-1, keepdims=True)
    o = o * lax.rsqrt(ms + NORM_EPS) * sg_ref[...]
    g = g_ref[...].astype(F32)
    o = (o * (1.0 - lambda_init)) * (g * jax.nn.sigmoid(g))
    o_ref[...] = o.astype(o_ref.dtype)


def _diff_attention(proj, lq1, lk1, lq2, lk2, subln_g, lambda_init, batch, seq):
    m = proj.shape[0]
    tq, tk = ATT_TQ, ATT_TK
    assert tq == tk
    nq = seq // tq
    wblk = DIFF_DV
    per_region = DIFF_HEADS

    vec = pl.BlockSpec((1, HEAD), lambda b, h, i: (0, 0))
    return pl.pallas_call(
        functools.partial(_attn_kernel, tq=tq, tk=tk, lambda_init=lambda_init),
        grid=(batch, DIFF_HEADS, nq),
        in_specs=[
            pl.BlockSpec((tq, wblk), lambda b, h, i: (b * nq + i, h)),
            pl.BlockSpec((seq, wblk), lambda b, h, i: (b, per_region + h)),
            pl.BlockSpec((seq, wblk), lambda b, h, i: (b, 2 * per_region + h)),
            pl.BlockSpec((tq, wblk), lambda b, h, i: (b * nq + i, 3 * per_region + h)),
            vec, vec, vec, vec,
            pl.BlockSpec((1, wblk), lambda b, h, i: (0, 0)),
        ],
        out_specs=pl.BlockSpec((tq, wblk), lambda b, h, i: (b * nq + i, h)),
        out_shape=jax.ShapeDtypeStruct((m, DIFF_HEADS * wblk), BF16),
        scratch_shapes=[
            pltpu.VMEM((2, tq, 1), F32),
            pltpu.VMEM((2, tq, 1), F32),
            pltpu.VMEM((2, tq, wblk), F32),
        ],
        compiler_params=_params(("arbitrary", "arbitrary", "arbitrary")),
        name="diff_attn",
    )(proj, proj, proj, proj, lq1.reshape(1, HEAD), lk1.reshape(1, HEAD),
      lq2.reshape(1, HEAD), lk2.reshape(1, HEAD), subln_g.reshape(1, wblk))


def _out_kernel(*refs, n_in):
    acts = refs[:n_in]
    ws = refs[n_in:2 * n_in]
    h_ref, g_ref, b_ref, o_ref, ob_ref = refs[2 * n_in:]
    y = jnp.dot(acts[0][...], ws[0][...], preferred_element_type=F32)
    for a, w in zip(acts[1:], ws[1:]):
        y = y + jnp.dot(a[...], w[...], preferred_element_type=F32)
    z = DEEPNORM_ALPHA * h_ref[...] + y
    mu = jnp.mean(z, axis=-1, keepdims=True)
    d = z - mu
    var = jnp.mean(d * d, axis=-1, keepdims=True)
    out = d * lax.rsqrt(var + NORM_EPS) * g_ref[...] + b_ref[...]
    o_ref[...] = out
    ob_ref[...] = out.astype(BF16)


def _out_proj_norm(acts, ws, h, ln_g, ln_b):
    m, d = h.shape
    tm = OUT_TM
    n_in = len(acts)
    in_specs = [pl.BlockSpec((tm, a.shape[1]), lambda i: (i, 0)) for a in acts]
    in_specs += [pl.BlockSpec(w.shape, lambda i: (0, 0)) for w in ws]
    in_specs += [pl.BlockSpec((tm, d), lambda i: (i, 0)),
                 pl.BlockSpec((1, d), lambda i: (0, 0)),
                 pl.BlockSpec((1, d), lambda i: (0, 0))]
    return pl.pallas_call(
        functools.partial(_out_kernel, n_in=n_in),
        grid=(m // tm,),
        in_specs=in_specs,
        out_specs=[pl.BlockSpec((tm, d), lambda i: (i, 0)),
                   pl.BlockSpec((tm, d), lambda i: (i, 0))],
        out_shape=[jax.ShapeDtypeStruct((m, d), F32), jax.ShapeDtypeStruct((m, d), BF16)],
        compiler_params=_params(("arbitrary",)),
        name="out_norm",
    )(*acts, *ws, h, ln_g.reshape(1, d), ln_b.reshape(1, d))


def _rope_tables(seq, scale_first, scale_second):
    half = HEAD // 2
    inv_freq = ROPE_THETA ** (-jnp.arange(half, dtype=F32) / half)
    ang = jnp.arange(seq, dtype=F32)[:, None] * inv_freq[None, :]
    cos = jnp.cos(ang)
    sin = jnp.sin(ang)
    cos_full = jnp.concatenate([cos, cos], axis=-1)
    sin_full = jnp.concatenate([-sin, sin], axis=-1)
    scales = jnp.array([scale_first, scale_second], F32)[:, None, None]
    return cos_full[None] * scales, sin_full[None] * scales


def kernel(x, ev_w_in, ev_conv_w, ev_conv_b, ev_gate_a_w, ev_gate_a_b, ev_gate_x_w, ev_gate_x_b,
           ev_lru_lambda, ev_ret_gn_g, ev_ret_gn_b, ev_w_out, ev_ln_g, ev_ln_b,
           od_w_in, od_lambda_q1, od_lambda_k1, od_lambda_q2, od_lambda_k2, od_subln_g, od_w_out,
           od_ln_g, od_ln_b):
    batch, seq, d = x.shape
    m = batch * seq
    ret_width = RET_HEADS * HEAD
    ev_cos, ev_sin = _rope_tables(seq, 1.0, HEAD ** -0.5)
    od_cos, od_sin = _rope_tables(seq, HEAD ** -0.5, 1.0)

    h = x.reshape(m, d).astype(F32)
    hb = h.astype(BF16)
    for l in range(DEPTH):
        j = l // 2
        if l % 2 == 0:
            proj = _project(hb, ev_w_in[j].astype(BF16), ev_cos, ev_sin, 2 * ret_width, seq)
            ret = _retention(proj, ev_ret_gn_g[j], ev_ret_gn_b[j], batch, seq)
            lru = _rg_lru(proj, ev_conv_w[j], ev_conv_b[j], ev_gate_a_w[j], ev_gate_a_b[j],
                          ev_gate_x_w[j], ev_gate_x_b[j], ev_lru_lambda[j], batch, seq)
            w_out = ev_w_out[j].astype(BF16)
            h, hb = _out_proj_norm([ret, lru], [w_out[:ret_width], w_out[ret_width:]],
                                   h, ev_ln_g[j], ev_ln_b[j])
        else:
            lambda_init = 0.8 - 0.6 * math.exp(-0.3 * l)
            proj = _project(hb, od_w_in[j].astype(BF16), od_cos, od_sin,
                            2 * 2 * DIFF_HEADS * HEAD, seq)
            mix = _diff_attention(proj, od_lambda_q1[j], od_lambda_k1[j], od_lambda_q2[j],
                                  od_lambda_k2[j], od_subln_g[j], lambda_init, batch, seq)
            h, hb = _out_proj_norm([mix], [od_w_out[j].astype(BF16)], h, od_ln_g[j], od_ln_b[j])
    return h.reshape(batch, seq, d).astype(x.dtype)
```

```python
import functools
import math

import jax
import jax.numpy as jnp
from jax import lax
from jax.experimental import pallas as pl
from jax.experimental.pallas import tpu as pltpu

F32 = jnp.float32
BF16 = jnp.bfloat16

D_MODEL = 1024
DEPTH = 4
ROPE_THETA = 10000.0
CHUNK = 128
NORM_EPS = 1e-5
HEAD = 128

RET_HEADS = 8
LRU_WIDTH = 1024
LRU_BLOCKS = 8
CONV_WIDTH = 4
LRU_C = 8.0
DIFF_HEADS = 8
DIFF_DV = 2 * HEAD

DEEPNORM_ALPHA = (2.0 * DEPTH) ** 0.25
LOG2E = math.log2(math.e)

VMEM_LIMIT = 48 * 1024 * 1024

PROJ_TM = 512
PROJ_TN = 1024
OUT_TM = 512
LRU_T = 256
ATT_TQ = 512
ATT_TK = 512


def _params(sem):
    return pltpu.CompilerParams(dimension_semantics=sem, vmem_limit_bytes=VMEM_LIMIT)


def _proj_kernel(x_ref, w_ref, cos_ref, sin_ref, o_ref, *, n_rope_blocks, tn):
    j = pl.program_id(0)
    acc = jnp.dot(x_ref[...], w_ref[...], preferred_element_type=F32)

    @pl.when(j < n_rope_blocks)
    def _():
        cos = cos_ref[0]
        sin = sin_ref[0]
        for h in range(tn // HEAD):
            a = acc[:, h * HEAD:(h + 1) * HEAD]
            r = a * cos + pltpu.roll(a, HEAD // 2, 1) * sin
            o_ref[:, h * HEAD:(h + 1) * HEAD] = r.astype(o_ref.dtype)

    @pl.when(j >= n_rope_blocks)
    def _():
        o_ref[...] = acc.astype(o_ref.dtype)


def _project(xb, w, cos_tab, sin_tab, rope_cols, seq):
    m, k = xb.shape
    n = w.shape[1]
    tm, tn = PROJ_TM, PROJ_TN
    n_rope_blocks = rope_cols // tn
    half_blocks = n_rope_blocks // 2
    seq_blocks = seq // tm

    def tab_map(j, i):
        return (jnp.minimum(j // half_blocks, 1), i % seq_blocks, 0)

    return pl.pallas_call(
        functools.partial(_proj_kernel, n_rope_blocks=n_rope_blocks, tn=tn),
        grid=(n // tn, m // tm),
        in_specs=[
            pl.BlockSpec((tm, k), lambda j, i: (i, 0)),
            pl.BlockSpec((k, tn), lambda j, i: (0, j)),
            pl.BlockSpec((1, tm, HEAD), tab_map),
            pl.BlockSpec((1, tm, HEAD), tab_map),
        ],
        out_specs=pl.BlockSpec((tm, tn), lambda j, i: (i, j)),
        out_shape=jax.ShapeDtypeStruct((m, n), BF16),
        compiler_params=_params(("arbitrary", "arbitrary")),
        name="proj",
    )(xb, w, cos_tab, sin_tab)


def _retention_kernel(cd_ref, q_ref, k_ref, v_ref, g_ref, decay_ref, qdec_ref, kend_ref,
                      gng_ref, gnb_ref, o_ref, state_ref):
    n = pl.program_id(1)

    @pl.when(n == 0)
    def _():
        state_ref[...] = jnp.zeros_like(state_ref)

    for h in range(RET_HEADS):
        hs = slice(h * HEAD, (h + 1) * HEAD)
        q = q_ref[:, hs]
        k = k_ref[:, hs]
        v = v_ref[:, hs]
        s = lax.dot_general(q, k, (((1,), (1,)), ((), ())), preferred_element_type=F32)
        s = s * decay_ref[h]
        intra = jnp.dot(s.astype(BF16), v, preferred_element_type=F32)
        state = state_ref[h]
        cross = jnp.dot(q, state.astype(BF16), preferred_element_type=F32) * qdec_ref[h]
        v_end = (v.astype(F32) * kend_ref[h]).astype(BF16)
        kv = lax.dot_general(k, v_end, (((0,), (0,)), ((), ())), preferred_element_type=F32)
        state_ref[h] = state * cd_ref[h] + kv
        o = intra + cross
        mu = jnp.mean(o, axis=-1, keepdims=True)
        d = o - mu
        var = jnp.mean(d * d, axis=-1, keepdims=True)
        o = d * lax.rsqrt(var + NORM_EPS) * gng_ref[:, hs] + gnb_ref[:, hs]
        g = g_ref[:, hs].astype(F32)
        o = o * (g * jax.nn.sigmoid(g))
        o_ref[:, hs] = o.astype(o_ref.dtype)


def _retention(proj, gn_g, gn_b, batch, seq):
    m = proj.shape[0]
    nchunk = seq // CHUNK
    width = RET_HEADS * HEAD
    log_g = jnp.log(1.0 - 2.0 ** (-5.0 - jnp.arange(RET_HEADS, dtype=F32)))
    idx = jnp.arange(CHUNK, dtype=F32)
    rel = idx[:, None] - idx[None, :]
    decay = jnp.where(rel[None] >= 0,
                      jnp.exp(log_g[:, None, None] * jnp.maximum(rel, 0.0)[None]), 0.0)
    kend = jnp.exp(log_g[:, None] * (CHUNK - 1 - idx)[None, :])
    qdec = jnp.exp(log_g[:, None] * (idx + 1.0)[None, :])
    kend = jnp.broadcast_to(kend[:, :, None], (RET_HEADS, CHUNK, HEAD))
    qdec = jnp.broadcast_to(qdec[:, :, None], (RET_HEADS, CHUNK, HEAD))
    chunk_decay = jnp.exp(log_g * CHUNK)

    def row(b, n):
        return b * nchunk + n

    def col_spec(c):
        return pl.BlockSpec((CHUNK, width), lambda b, n, c=c: (row(b, n), c))

    table = pl.BlockSpec((RET_HEADS, CHUNK, HEAD), lambda b, n: (0, 0, 0))
    vec = pl.BlockSpec((1, width), lambda b, n: (0, 0))
    return pl.pallas_call(
        _retention_kernel,
        grid=(batch, nchunk),
        in_specs=[
            pl.BlockSpec(memory_space=pltpu.SMEM),
            col_spec(0), col_spec(1), col_spec(2), col_spec(3),
            table, table, table, vec, vec,
        ],
        out_specs=pl.BlockSpec((CHUNK, width), lambda b, n: (row(b, n), 0)),
        out_shape=jax.ShapeDtypeStruct((m, width), BF16),
        scratch_shapes=[pltpu.VMEM((RET_HEADS, HEAD, HEAD), F32)],
        compiler_params=_params(("arbitrary", "arbitrary")),
        name="retention",
    )(chunk_decay, proj, proj, proj, proj, decay, qdec, kend,
      gn_g.reshape(1, width), gn_b.reshape(1, width))


def _softplus(z):
    return jnp.maximum(z, 0.0) + jnp.log1p(jnp.exp(-jnp.abs(z)))


def _lru_kernel(x_ref, g_ref, cw_ref, cb_ref, wa_ref, ba_ref, wx_ref, bx_ref, lam_ref,
                o_ref, xpad_ref, a_ref, u_ref, h_ref, carry_ref, *, t_blk):
    t = pl.program_id(1)
    pad = 8

    @pl.when(t == 0)
    def _():
        xpad_ref[0:pad, :] = jnp.zeros((pad, LRU_WIDTH), F32)
        carry_ref[...] = jnp.zeros_like(carry_ref)

    xpad_ref[pad:pad + t_blk, :] = x_ref[...].astype(F32)
    xc = cb_ref[...] + jnp.zeros((t_blk, LRU_WIDTH), F32)
    for w in range(CONV_WIDTH):
        off = pad - (CONV_WIDTH - 1) + w
        xc = xc + cw_ref[w:w + 1, :] * xpad_ref[off:off + t_blk, :]
    xpad_ref[0:pad, :] = xpad_ref[t_blk:t_blk + pad, :]

    neg_c_sp = -LRU_C * _softplus(-lam_ref[...])
    bd = LRU_WIDTH // LRU_BLOCKS
    for blk in range(LRU_BLOCKS):
        cs = slice(blk * bd, (blk + 1) * bd)
        xg = xc[:, cs]
        xgb = xg.astype(BF16)
        r = jax.nn.sigmoid(jnp.dot(xgb, wa_ref[blk], preferred_element_type=F32) + ba_ref[:, cs])
        i = jax.nn.sigmoid(jnp.dot(xgb, wx_ref[blk], preferred_element_type=F32) + bx_ref[:, cs])
        log_a = r * neg_c_sp[:, cs]
        a = jnp.exp(log_a)
        a_ref[:, cs] = a
        u_ref[:, cs] = xg * i * jnp.sqrt(-jnp.tanh(log_a) * (a * a + 1.0))

    def body(i, h):
        r0 = pl.multiple_of(i * 8, 8)
        a8 = a_ref[pl.ds(r0, 8), :]
        u8 = u_ref[pl.ds(r0, 8), :]
        rows = []
        for j in range(8):
            h = a8[j:j + 1, :] * h + u8[j:j + 1, :]
            rows.append(h)
        h_ref[pl.ds(r0, 8), :] = jnp.concatenate(rows, axis=0)
        return h

    carry_ref[...] = lax.fori_loop(0, t_blk // 8, body, carry_ref[...])
    g = g_ref[...].astype(F32)
    o_ref[...] = (h_ref[...] * (g * jax.nn.sigmoid(g))).astype(o_ref.dtype)


def _rg_lru(proj, conv_w, conv_b, wa, ba, wx, bx, lam, batch, seq):
    m = proj.shape[0]
    t_blk = LRU_T
    nblk = seq // t_blk
    w = LRU_WIDTH
    bd = w // LRU_BLOCKS

    def col_spec(c):
        return pl.BlockSpec((t_blk, w), lambda b, t, c=c: (b * nblk + t, c))

    vec = pl.BlockSpec((1, w), lambda b, t: (0, 0))
    gate_w = pl.BlockSpec((LRU_BLOCKS, bd, bd), lambda b, t: (0, 0, 0))
    return pl.pallas_call(
        functools.partial(_lru_kernel, t_blk=t_blk),
        grid=(batch, nblk),
        in_specs=[
            col_spec(4), col_spec(5),
            pl.BlockSpec((CONV_WIDTH, w), lambda b, t: (0, 0)), vec,
            gate_w, vec, gate_w, vec, vec,
        ],
        out_specs=pl.BlockSpec((t_blk, w), lambda b, t: (b * nblk + t, 0)),
        out_shape=jax.ShapeDtypeStruct((m, w), BF16),
        scratch_shapes=[
            pltpu.VMEM((t_blk + 8, w), F32),
            pltpu.VMEM((t_blk, w), F32),
            pltpu.VMEM((t_blk, w), F32),
            pltpu.VMEM((t_blk, w), F32),
            pltpu.VMEM((1, w), F32),
        ],
        compiler_params=_params(("arbitrary", "arbitrary")),
        name="rg_lru",
    )(proj, proj, conv_w, conv_b.reshape(1, w), wa.astype(BF16), ba.reshape(1, w),
      wx.astype(BF16), bx.reshape(1, w), lam.reshape(1, w))


def _attn_kernel(q_ref, k_ref, v_ref, g_ref, bias_ref, lq1_ref, lk1_ref, lq2_ref, lk2_ref,
                 sg_ref, o_ref, vt_ref, acc_ref, *, seq, lambda_init):
    tq = ATT_TQ
    tk = ATT_TK
    for c in range(seq // tk):
        vt_ref[c] = v_ref[c * tk:(c + 1) * tk, :].T

    lam = (jnp.exp(jnp.sum(lq1_ref[...] * lk1_ref[...], axis=-1, keepdims=True))
           - jnp.exp(jnp.sum(lq2_ref[...] * lk2_ref[...], axis=-1, keepdims=True))
           + lambda_init)

    def scores(qi, j):
        qblk = q_ref[qi * tq:(qi + 1) * tq, :]
        kblk = k_ref[j * tk:(j + 1) * tk, :]
        return [lax.dot_general(kblk[:, c * HEAD:(c + 1) * HEAD], qblk[:, c * HEAD:(c + 1) * HEAD],
                                (((1,), (1,)), ((), ())), preferred_element_type=F32)
                for c in range(2)]

    for qi in range(seq // tq):
        s_next = scores(qi, 0)
        m = [None, None]
        l = [None, None]
        for j in range(qi + 1):
            s_cur = s_next
            if j < qi:
                s_next = scores(qi, j + 1)
            for c in range(2):
                s = s_cur[c]
                if j == qi:
                    s = s + bias_ref[...]
                mx = jnp.max(s, axis=0, keepdims=True)
                if j == 0:
                    m_new = mx
                    p = jnp.exp2(s - m_new)
                    l[c] = jnp.sum(p, axis=0, keepdims=True)
                    acc_ref[c] = jnp.dot(vt_ref[j], p.astype(BF16), preferred_element_type=F32)
                else:
                    m_new = jnp.maximum(m[c], mx)
                    alpha = jnp.exp2(m[c] - m_new)
                    p = jnp.exp2(s - m_new)
                    l[c] = alpha * l[c] + jnp.sum(p, axis=0, keepdims=True)
                    pv = jnp.dot(vt_ref[j], p.astype(BF16), preferred_element_type=F32)
                    acc_ref[c] = alpha * acc_ref[c] + pv
                m[c] = m_new

        o_t = acc_ref[0] * (1.0 / l[0]) - lam * (acc_ref[1] * (1.0 / l[1]))
        ms = jnp.mean(o_t * o_t, axis=0, keepdims=True)
        o = (o_t * lax.rsqrt(ms + NORM_EPS)).T
        g = g_ref[qi * tq:(qi + 1) * tq, :].astype(F32)
        o = (o * sg_ref[...]) * (1.0 - lambda_init) * (g * jax.nn.sigmoid(g))
        o_ref[qi * tq:(qi + 1) * tq, :] = o.astype(o_ref.dtype)


def _diff_attention(proj, lq1, lk1, lq2, lk2, subln_g, lambda_init, batch, seq):
    m = proj.shape[0]
    tq, tk = ATT_TQ, ATT_TK
    assert tq == tk and seq % tk == 0
    wblk = DIFF_DV
    per_region = DIFF_HEADS
    kk = jnp.arange(tk, dtype=jnp.int32)[:, None]
    qq = jnp.arange(tq, dtype=jnp.int32)[None, :]
    bias = jnp.where(kk <= qq, 0.0, -jnp.inf).astype(F32)

    def blk(region):
        return pl.BlockSpec((seq, wblk), lambda b, h, r=region: (b, r * per_region + h))

    vec = pl.BlockSpec((1, HEAD), lambda b, h: (0, 0))
    return pl.pallas_call(
        functools.partial(_attn_kernel, seq=seq, lambda_init=lambda_init),
        grid=(batch, DIFF_HEADS),
        in_specs=[
            blk(0), blk(1), blk(2), blk(3),
            pl.BlockSpec((tk, tq), lambda b, h: (0, 0)),
            vec, vec, vec, vec,
            pl.BlockSpec((1, wblk), lambda b, h: (0, 0)),
        ],
        out_specs=pl.BlockSpec((seq, wblk), lambda b, h: (b, h)),
        out_shape=jax.ShapeDtypeStruct((m, DIFF_HEADS * wblk), BF16),
        scratch_shapes=[
            pltpu.VMEM((seq // tk, wblk, tk), BF16),
            pltpu.VMEM((2, wblk, tq), F32),
        ],
        compiler_params=_params(("arbitrary", "arbitrary")),
        name="diff_attn",
    )(proj, proj, proj, proj, bias, lq1.reshape(1, HEAD), lk1.reshape(1, HEAD),
      lq2.reshape(1, HEAD), lk2.reshape(1, HEAD), subln_g.reshape(1, wblk))


def _out_kernel(*refs, n_in):
    acts = refs[:n_in]
    ws = refs[n_in:2 * n_in]
    h_ref, g_ref, b_ref, o_ref, ob_ref = refs[2 * n_in:]
    y = jnp.dot(acts[0][...], ws[0][...], preferred_element_type=F32)
    for a, w in zip(acts[1:], ws[1:]):
        y = y + jnp.dot(a[...], w[...], preferred_element_type=F32)
    z = DEEPNORM_ALPHA * h_ref[...] + y
    mu = jnp.mean(z, axis=-1, keepdims=True)
    d = z - mu
    var = jnp.mean(d * d, axis=-1, keepdims=True)
    out = d * lax.rsqrt(var + NORM_EPS) * g_ref[...] + b_ref[...]
    o_ref[...] = out
    ob_ref[...] = out.astype(BF16)


def _out_proj_norm(acts, ws, h, ln_g, ln_b):
    m, d = h.shape
    tm = OUT_TM
    n_in = len(acts)
    in_specs = [pl.BlockSpec((tm, a.shape[1]), lambda i: (i, 0)) for a in acts]
    in_specs += [pl.BlockSpec(w.shape, lambda i: (0, 0)) for w in ws]
    in_specs += [pl.BlockSpec((tm, d), lambda i: (i, 0)),
                 pl.BlockSpec((1, d), lambda i: (0, 0)),
                 pl.BlockSpec((1, d), lambda i: (0, 0))]
    return pl.pallas_call(
        functools.partial(_out_kernel, n_in=n_in),
        grid=(m // tm,),
        in_specs=in_specs,
        out_specs=[pl.BlockSpec((tm, d), lambda i: (i, 0)),
                   pl.BlockSpec((tm, d), lambda i: (i, 0))],
        out_shape=[jax.ShapeDtypeStruct((m, d), F32), jax.ShapeDtypeStruct((m, d), BF16)],
        compiler_params=_params(("arbitrary",)),
        name="out_norm",
    )(*acts, *ws, h, ln_g.reshape(1, d), ln_b.reshape(1, d))


def _rope_tables(seq, scale_first, scale_second):
    half = HEAD // 2
    inv_freq = ROPE_THETA ** (-jnp.arange(half, dtype=F32) / half)
    ang = jnp.arange(seq, dtype=F32)[:, None] * inv_freq[None, :]
    cos = jnp.cos(ang)
    sin = jnp.sin(ang)
    cos_full = jnp.concatenate([cos, cos], axis=-1)
    sin_full = jnp.concatenate([-sin, sin], axis=-1)
    scales = jnp.array([scale_first, scale_second], F32)[:, None, None]
    return cos_full[None] * scales, sin_full[None] * scales


def kernel(x, ev_w_in, ev_conv_w, ev_conv_b, ev_gate_a_w, ev_gate_a_b, ev_gate_x_w, ev_gate_x_b,
           ev_lru_lambda, ev_ret_gn_g, ev_ret_gn_b, ev_w_out, ev_ln_g, ev_ln_b,
           od_w_in, od_lambda_q1, od_lambda_k1, od_lambda_q2, od_lambda_k2, od_subln_g, od_w_out,
           od_ln_g, od_ln_b):
    batch, seq, d = x.shape
    m = batch * seq
    ret_width = RET_HEADS * HEAD
    ev_cos, ev_sin = _rope_tables(seq, 1.0, HEAD ** -0.5)
    od_cos, od_sin = _rope_tables(seq, HEAD ** -0.5 * LOG2E, 1.0)

    h = x.reshape(m, d).astype(F32)
    hb = h.astype(BF16)
    for l in range(DEPTH):
        j = l // 2
        if l % 2 == 0:
            proj = _project(hb, ev_w_in[j].astype(BF16), ev_cos, ev_sin, 2 * ret_width, seq)
            ret = _retention(proj, ev_ret_gn_g[j], ev_ret_gn_b[j], batch, seq)
            lru = _rg_lru(proj, ev_conv_w[j], ev_conv_b[j], ev_gate_a_w[j], ev_gate_a_b[j],
                          ev_gate_x_w[j], ev_gate_x_b[j], ev_lru_lambda[j], batch, seq)
            w_out = ev_w_out[j].astype(BF16)
            h, hb = _out_proj_norm([ret, lru], [w_out[:ret_width], w_out[ret_width:]],
                                   h, ev_ln_g[j], ev_ln_b[j])
        else:
            lambda_init = 0.8 - 0.6 * math.exp(-0.3 * l)
            proj = _project(hb, od_w_in[j].astype(BF16), od_cos, od_sin,
                            2 * 2 * DIFF_HEADS * HEAD, seq)
            mix = _diff_attention(proj, od_lambda_q1[j], od_lambda_k1[j], od_lambda_q2[j],
                                  od_lambda_k2[j], od_subln_g[j], lambda_init, batch, seq)
            h, hb = _out_proj_norm([mix], [od_w_out[j].astype(BF16)], h, od_ln_g[j], od_ln_b[j])
    return h.reshape(batch, seq, d).astype(x.dtype)
```

```python
import functools
import math

import jax
import jax.numpy as jnp
from jax import lax
from jax.experimental import pallas as pl
from jax.experimental.pallas import tpu as pltpu

F32 = jnp.float32
BF16 = jnp.bfloat16

D_MODEL = 1024
DEPTH = 4
ROPE_THETA = 10000.0
CHUNK = 128
NORM_EPS = 1e-5
HEAD = 128

RET_HEADS = 8
LRU_WIDTH = 1024
LRU_BLOCKS = 8
CONV_WIDTH = 4
LRU_C = 8.0
DIFF_HEADS = 8
DIFF_DV = 2 * HEAD

DEEPNORM_ALPHA = (2.0 * DEPTH) ** 0.25
LOG2E = math.log2(math.e)

VMEM_LIMIT = 48 * 1024 * 1024

PROJ_TM = 1024
PROJ_TN = 2048
PROJ_CHUNK = 512
OUT_TM = 1024
OUT_CHUNK = 256
LRU_T = 256
ATT_TQ = 512
ATT_TK = 512
ATT_ONES_ROWS = 16


def _params(sem):
    return pltpu.CompilerParams(dimension_semantics=sem, vmem_limit_bytes=VMEM_LIMIT)


def _proj_kernel(x_ref, w_ref, cos_ref, sin_ref, o_ref, *, n_rope_blocks, half_chunks, tn):
    j = pl.program_id(0)
    chunks = tn // PROJ_CHUNK

    @pl.when(j < n_rope_blocks)
    def _():
        for c in range(chunks):
            c0 = c * PROJ_CHUNK
            acc = jnp.dot(x_ref[...], w_ref[:, c0:c0 + PROJ_CHUNK], preferred_element_type=F32)
            tab = (j * chunks + c) // half_chunks
            cos = cos_ref[tab]
            sin = sin_ref[tab]
            for h in range(PROJ_CHUNK // HEAD):
                a = acc[:, h * HEAD:(h + 1) * HEAD]
                r = a * cos + pltpu.roll(a, HEAD // 2, 1) * sin
                o_ref[:, c0 + h * HEAD:c0 + (h + 1) * HEAD] = r.astype(o_ref.dtype)

    @pl.when(j >= n_rope_blocks)
    def _():
        for c in range(chunks):
            c0 = c * PROJ_CHUNK
            acc = jnp.dot(x_ref[...], w_ref[:, c0:c0 + PROJ_CHUNK], preferred_element_type=F32)
            o_ref[:, c0:c0 + PROJ_CHUNK] = acc.astype(o_ref.dtype)


def _project(xb, w, cos_tab, sin_tab, rope_cols, seq):
    m, k = xb.shape
    n = w.shape[1]
    tm, tn = min(PROJ_TM, seq), PROJ_TN
    assert rope_cols % tn == 0 and (rope_cols // 2) % PROJ_CHUNK == 0 and seq % tm == 0
    seq_blocks = seq // tm

    def tab_map(j, i):
        return (0, i % seq_blocks, 0)

    return pl.pallas_call(
        functools.partial(_proj_kernel, n_rope_blocks=rope_cols // tn,
                          half_chunks=rope_cols // 2 // PROJ_CHUNK, tn=tn),
        grid=(n // tn, m // tm),
        in_specs=[
            pl.BlockSpec((tm, k), lambda j, i: (i, 0)),
            pl.BlockSpec((k, tn), lambda j, i: (0, j)),
            pl.BlockSpec((2, tm, HEAD), tab_map),
            pl.BlockSpec((2, tm, HEAD), tab_map),
        ],
        out_specs=pl.BlockSpec((tm, tn), lambda j, i: (i, j)),
        out_shape=jax.ShapeDtypeStruct((m, n), BF16),
        compiler_params=_params(("arbitrary", "arbitrary")),
        name="proj",
    )(xb, w, cos_tab, sin_tab)


def _retention_kernel(cd_ref, q_ref, k_ref, v_ref, g_ref, decay_ref, qdec_ref, kend_ref,
                      gng_ref, gnb_ref, o_ref, state_ref):
    n = pl.program_id(1)

    @pl.when(n == 0)
    def _():
        state_ref[...] = jnp.zeros_like(state_ref)

    for h in range(RET_HEADS):
        hs = slice(h * HEAD, (h + 1) * HEAD)
        q = q_ref[:, hs]
        k = k_ref[:, hs]
        v = v_ref[:, hs]
        s = lax.dot_general(q, k, (((1,), (1,)), ((), ())), preferred_element_type=F32)
        s = s * decay_ref[h]
        intra = jnp.dot(s.astype(BF16), v, preferred_element_type=F32)
        state = state_ref[h]
        cross = jnp.dot(q, state.astype(BF16), preferred_element_type=F32) * qdec_ref[h]
        v_end = (v.astype(F32) * kend_ref[h]).astype(BF16)
        kv = lax.dot_general(k, v_end, (((0,), (0,)), ((), ())), preferred_element_type=F32)
        state_ref[h] = state * cd_ref[h] + kv
        o = intra + cross
        mu = jnp.mean(o, axis=-1, keepdims=True)
        d = o - mu
        var = jnp.mean(d * d, axis=-1, keepdims=True)
        o = d * lax.rsqrt(var + NORM_EPS) * gng_ref[:, hs] + gnb_ref[:, hs]
        g = g_ref[:, hs].astype(F32)
        o = o * (g * jax.nn.sigmoid(g))
        o_ref[:, hs] = o.astype(o_ref.dtype)


def _retention(proj, gn_g, gn_b, batch, seq):
    m = proj.shape[0]
    nchunk = seq // CHUNK
    width = RET_HEADS * HEAD
    log_g = jnp.log(1.0 - 2.0 ** (-5.0 - jnp.arange(RET_HEADS, dtype=F32)))
    idx = jnp.arange(CHUNK, dtype=F32)
    rel = idx[:, None] - idx[None, :]
    decay = jnp.where(rel[None] >= 0,
                      jnp.exp(log_g[:, None, None] * jnp.maximum(rel, 0.0)[None]), 0.0)
    kend = jnp.exp(log_g[:, None] * (CHUNK - 1 - idx)[None, :])
    qdec = jnp.exp(log_g[:, None] * (idx + 1.0)[None, :])
    kend = jnp.broadcast_to(kend[:, :, None], (RET_HEADS, CHUNK, HEAD))
    qdec = jnp.broadcast_to(qdec[:, :, None], (RET_HEADS, CHUNK, HEAD))
    chunk_decay = jnp.exp(log_g * CHUNK)

    def row(b, n):
        return b * nchunk + n

    def col_spec(c):
        return pl.BlockSpec((CHUNK, width), lambda b, n, c=c: (row(b, n), c))

    table = pl.BlockSpec((RET_HEADS, CHUNK, HEAD), lambda b, n: (0, 0, 0))
    vec = pl.BlockSpec((1, width), lambda b, n: (0, 0))
    return pl.pallas_call(
        _retention_kernel,
        grid=(batch, nchunk),
        in_specs=[
            pl.BlockSpec(memory_space=pltpu.SMEM),
            col_spec(0), col_spec(1), col_spec(2), col_spec(3),
            table, table, table, vec, vec,
        ],
        out_specs=pl.BlockSpec((CHUNK, width), lambda b, n: (row(b, n), 0)),
        out_shape=jax.ShapeDtypeStruct((m, width), BF16),
        scratch_shapes=[pltpu.VMEM((RET_HEADS, HEAD, HEAD), F32)],
        compiler_params=_params(("arbitrary", "arbitrary")),
        name="retention",
    )(chunk_decay, proj, proj, proj, proj, decay, qdec, kend,
      gn_g.reshape(1, width), gn_b.reshape(1, width))


def _softplus(z):
    return jnp.maximum(z, 0.0) + jnp.log1p(jnp.exp(-jnp.abs(z)))


def _lru_kernel(x_ref, g_ref, cw_ref, cb_ref, wa_ref, ba_ref, wx_ref, bx_ref, lam_ref,
                o_ref, xpad_ref, a_ref, u_ref, h_ref, carry_ref, *, t_blk):
    t = pl.program_id(1)
    pad = 8

    @pl.when(t == 0)
    def _():
        xpad_ref[0:pad, :] = jnp.zeros((pad, LRU_WIDTH), F32)
        carry_ref[...] = jnp.zeros_like(carry_ref)

    xpad_ref[pad:pad + t_blk, :] = x_ref[...].astype(F32)
    xc = cb_ref[...] + jnp.zeros((t_blk, LRU_WIDTH), F32)
    for w in range(CONV_WIDTH):
        off = pad - (CONV_WIDTH - 1) + w
        xc = xc + cw_ref[w:w + 1, :] * xpad_ref[off:off + t_blk, :]
    xpad_ref[0:pad, :] = xpad_ref[t_blk:t_blk + pad, :]

    neg_c_sp = -LRU_C * _softplus(-lam_ref[...])
    bd = LRU_WIDTH // LRU_BLOCKS
    for blk in range(LRU_BLOCKS):
        cs = slice(blk * bd, (blk + 1) * bd)
        xg = xc[:, cs]
        xgb = xg.astype(BF16)
        r = jax.nn.sigmoid(jnp.dot(xgb, wa_ref[blk], preferred_element_type=F32) + ba_ref[:, cs])
        i = jax.nn.sigmoid(jnp.dot(xgb, wx_ref[blk], preferred_element_type=F32) + bx_ref[:, cs])
        log_a = r * neg_c_sp[:, cs]
        a = jnp.exp(log_a)
        a_ref[:, cs] = a
        u_ref[:, cs] = xg * i * jnp.sqrt(-jnp.tanh(log_a) * (a * a + 1.0))

    def body(i, h):
        r0 = pl.multiple_of(i * 8, 8)
        a8 = a_ref[pl.ds(r0, 8), :]
        u8 = u_ref[pl.ds(r0, 8), :]
        rows = []
        for j in range(8):
            h = a8[j:j + 1, :] * h + u8[j:j + 1, :]
            rows.append(h)
        h_ref[pl.ds(r0, 8), :] = jnp.concatenate(rows, axis=0)
        return h

    carry_ref[...] = lax.fori_loop(0, t_blk // 8, body, carry_ref[...])
    g = g_ref[...].astype(F32)
    o_ref[...] = (h_ref[...] * (g * jax.nn.sigmoid(g))).astype(o_ref.dtype)


def _rg_lru(proj, conv_w, conv_b, wa, ba, wx, bx, lam, batch, seq):
    m = proj.shape[0]
    t_blk = LRU_T
    nblk = seq // t_blk
    w = LRU_WIDTH
    bd = w // LRU_BLOCKS

    def col_spec(c):
        return pl.BlockSpec((t_blk, w), lambda b, t, c=c: (b * nblk + t, c))

    vec = pl.BlockSpec((1, w), lambda b, t: (0, 0))
    gate_w = pl.BlockSpec((LRU_BLOCKS, bd, bd), lambda b, t: (0, 0, 0))
    return pl.pallas_call(
        functools.partial(_lru_kernel, t_blk=t_blk),
        grid=(batch, nblk),
        in_specs=[
            col_spec(4), col_spec(5),
            pl.BlockSpec((CONV_WIDTH, w), lambda b, t: (0, 0)), vec,
            gate_w, vec, gate_w, vec, vec,
        ],
        out_specs=pl.BlockSpec((t_blk, w), lambda b, t: (b * nblk + t, 0)),
        out_shape=jax.ShapeDtypeStruct((m, w), BF16),
        scratch_shapes=[
            pltpu.VMEM((t_blk + 8, w), F32),
            pltpu.VMEM((t_blk, w), F32),
            pltpu.VMEM((t_blk, w), F32),
            pltpu.VMEM((t_blk, w), F32),
            pltpu.VMEM((1, w), F32),
        ],
        compiler_params=_params(("arbitrary", "arbitrary")),
        name="rg_lru",
    )(proj, proj, conv_w, conv_b.reshape(1, w), wa.astype(BF16), ba.reshape(1, w),
      wx.astype(BF16), bx.reshape(1, w), lam.reshape(1, w))


def _attn_kernel(q_ref, k_ref, v_ref, g_ref, bias_ref, lq1_ref, lk1_ref, lq2_ref, lk2_ref,
                 sg_ref, o_ref, vt_ref, acc_ref, s_ref, *, seq, lambda_init):
    tq = ATT_TQ
    tk = ATT_TK
    for c in range(seq // tk):
        vt_ref[c, 0:DIFF_DV, :] = v_ref[c * tk:(c + 1) * tk, :].T
        vt_ref[c, DIFF_DV:, :] = jnp.ones((ATT_ONES_ROWS, tk), BF16)

    lam = (jnp.exp(jnp.sum(lq1_ref[...] * lk1_ref[...], axis=-1, keepdims=True))
           - jnp.exp(jnp.sum(lq2_ref[...] * lk2_ref[...], axis=-1, keepdims=True))
           + lambda_init)

    def scores(qi, j, slot):
        qblk = q_ref[qi * tq:(qi + 1) * tq, :]
        kblk = k_ref[j * tk:(j + 1) * tk, :]
        for c in range(2):
            s_ref[slot, c] = lax.dot_general(
                kblk[:, c * HEAD:(c + 1) * HEAD], qblk[:, c * HEAD:(c + 1) * HEAD],
                (((1,), (1,)), ((), ())), preferred_element_type=F32)

    step = 0
    for qi in range(seq // tq):
        scores(qi, 0, step % 2)
        m = [None, None]
        for j in range(qi + 1):
            slot = step % 2
            step += 1
            if j < qi:
                scores(qi, j + 1, step % 2)
            for c in range(2):
                s = s_ref[slot, c]
                if j == qi:
                    s = s + bias_ref[...]
                mx = jnp.max(s, axis=0, keepdims=True)
                m_new = mx if j == 0 else jnp.maximum(m[c], mx)
                p = jnp.exp2((s - m_new).astype(BF16))
                pv = jnp.dot(vt_ref[j], p, preferred_element_type=F32)
                if j == 0:
                    acc_ref[c] = pv
                else:
                    acc_ref[c] = jnp.exp2(m[c] - m_new) * acc_ref[c] + pv
                m[c] = m_new

        dv = DIFF_DV
        inv_l = [1.0 / acc_ref[c, dv:dv + 1, :] for c in range(2)]
        o_t = acc_ref[0, 0:dv, :] * inv_l[0] - lam * (acc_ref[1, 0:dv, :] * inv_l[1])
        ms = jnp.mean(o_t * o_t, axis=0, keepdims=True)
        o = (o_t * lax.rsqrt(ms + NORM_EPS)).T
        g = g_ref[qi * tq:(qi + 1) * tq, :].astype(F32)
        o = (o * sg_ref[...]) * (1.0 - lambda_init) * (g * jax.nn.sigmoid(g))
        o_ref[qi * tq:(qi + 1) * tq, :] = o.astype(o_ref.dtype)


def _diff_attention(proj, lq1, lk1, lq2, lk2, subln_g, lambda_init, batch, seq):
    m = proj.shape[0]
    tq, tk = ATT_TQ, ATT_TK
    assert tq == tk and seq % tk == 0
    wblk = DIFF_DV
    per_region = DIFF_HEADS
    kk = jnp.arange(tk, dtype=jnp.int32)[:, None]
    qq = jnp.arange(tq, dtype=jnp.int32)[None, :]
    bias = jnp.where(kk <= qq, 0.0, -jnp.inf).astype(F32)

    def blk(region):
        return pl.BlockSpec((seq, wblk), lambda b, h, r=region: (b, r * per_region + h))

    vec = pl.BlockSpec((1, HEAD), lambda b, h: (0, 0))
    return pl.pallas_call(
        functools.partial(_attn_kernel, seq=seq, lambda_init=lambda_init),
        grid=(batch, DIFF_HEADS),
        in_specs=[
            blk(0), blk(1), blk(2), blk(3),
            pl.BlockSpec((tk, tq), lambda b, h: (0, 0)),
            vec, vec, vec, vec,
            pl.BlockSpec((1, wblk), lambda b, h: (0, 0)),
        ],
        out_specs=pl.BlockSpec((seq, wblk), lambda b, h: (b, h)),
        out_shape=jax.ShapeDtypeStruct((m, DIFF_HEADS * wblk), BF16),
        scratch_shapes=[
            pltpu.VMEM((seq // tk, wblk + ATT_ONES_ROWS, tk), BF16),
            pltpu.VMEM((2, wblk + ATT_ONES_ROWS, tq), F32),
            pltpu.VMEM((2, 2, tk, tq), F32),
        ],
        compiler_params=_params(("arbitrary", "arbitrary")),
        name="diff_attn",
    )(proj, proj, proj, proj, bias, lq1.reshape(1, HEAD), lk1.reshape(1, HEAD),
      lq2.reshape(1, HEAD), lk2.reshape(1, HEAD), subln_g.reshape(1, wblk))


def _out_kernel(*refs, n_in):
    acts = refs[:n_in]
    ws = refs[n_in:2 * n_in]
    h_ref, g_ref, b_ref, o_ref, ob_ref = refs[2 * n_in:]
    for r in range(h_ref.shape[0] // OUT_CHUNK):
        rows = slice(r * OUT_CHUNK, (r + 1) * OUT_CHUNK)
        y = jnp.dot(acts[0][rows, :], ws[0][...], preferred_element_type=F32)
        for a, w in zip(acts[1:], ws[1:]):
            y = y + jnp.dot(a[rows, :], w[...], preferred_element_type=F32)
        z = DEEPNORM_ALPHA * h_ref[rows, :] + y
        mu = jnp.mean(z, axis=-1, keepdims=True)
        d = z - mu
        var = jnp.mean(d * d, axis=-1, keepdims=True)
        out = d * lax.rsqrt(var + NORM_EPS) * g_ref[...] + b_ref[...]
        o_ref[rows, :] = out
        ob_ref[rows, :] = out.astype(BF16)


def _out_proj_norm(acts, ws, h, ln_g, ln_b):
    m, d = h.shape
    tm = OUT_TM
    n_in = len(acts)
    in_specs = [pl.BlockSpec((tm, a.shape[1]), lambda i: (i, 0)) for a in acts]
    in_specs += [pl.BlockSpec(w.shape, lambda i: (0, 0)) for w in ws]
    in_specs += [pl.BlockSpec((tm, d), lambda i: (i, 0)),
                 pl.BlockSpec((1, d), lambda i: (0, 0)),
                 pl.BlockSpec((1, d), lambda i: (0, 0))]
    return pl.pallas_call(
        functools.partial(_out_kernel, n_in=n_in),
        grid=(m // tm,),
        in_specs=in_specs,
        out_specs=[pl.BlockSpec((tm, d), lambda i: (i, 0)),
                   pl.BlockSpec((tm, d), lambda i: (i, 0))],
        out_shape=[jax.ShapeDtypeStruct((m, d), F32), jax.ShapeDtypeStruct((m, d), BF16)],
        compiler_params=_params(("arbitrary",)),
        name="out_norm",
    )(*acts, *ws, h, ln_g.reshape(1, d), ln_b.reshape(1, d))


def _rope_tables(seq, scale_first, scale_second):
    half = HEAD // 2
    inv_freq = ROPE_THETA ** (-jnp.arange(half, dtype=F32) / half)
    ang = jnp.arange(seq, dtype=F32)[:, None] * inv_freq[None, :]
    cos = jnp.cos(ang)
    sin = jnp.sin(ang)
    cos_full = jnp.concatenate([cos, cos], axis=-1)
    sin_full = jnp.concatenate([-sin, sin], axis=-1)
    scales = jnp.array([scale_first, scale_second], F32)[:, None, None]
    return cos_full[None] * scales, sin_full[None] * scales


def kernel(x, ev_w_in, ev_conv_w, ev_conv_b, ev_gate_a_w, ev_gate_a_b, ev_gate_x_w, ev_gate_x_b,
           ev_lru_lambda, ev_ret_gn_g, ev_ret_gn_b, ev_w_out, ev_ln_g, ev_ln_b,
           od_w_in, od_lambda_q1, od_lambda_k1, od_lambda_q2, od_lambda_k2, od_subln_g, od_w_out,
           od_ln_g, od_ln_b):
    batch, seq, d = x.shape
    m = batch * seq
    ret_width = RET_HEADS * HEAD
    ev_cos, ev_sin = _rope_tables(seq, 1.0, HEAD ** -0.5)
    od_cos, od_sin = _rope_tables(seq, HEAD ** -0.5 * LOG2E, 1.0)

    h = x.reshape(m, d).astype(F32)
    hb = h.astype(BF16)
    for l in range(DEPTH):
        j = l // 2
        if l % 2 == 0:
            proj = _project(hb, ev_w_in[j].astype(BF16), ev_cos, ev_sin, 2 * ret_width, seq)
            ret = _retention(proj, ev_ret_gn_g[j], ev_ret_gn_b[j], batch, seq)
            lru = _rg_lru(proj, ev_conv_w[j], ev_conv_b[j], ev_gate_a_w[j], ev_gate_a_b[j],
                          ev_gate_x_w[j], ev_gate_x_b[j], ev_lru_lambda[j], batch, seq)
            w_out = ev_w_out[j].astype(BF16)
            h, hb = _out_proj_norm([ret, lru], [w_out[:ret_width], w_out[ret_width:]],
                                   h, ev_ln_g[j], ev_ln_b[j])
        else:
            lambda_init = 0.8 - 0.6 * math.exp(-0.3 * l)
            proj = _project(hb, od_w_in[j].astype(BF16), od_cos, od_sin,
                            2 * 2 * DIFF_HEADS * HEAD, seq)
            mix = _diff_attention(proj, od_lambda_q1[j], od_lambda_k1[j], od_lambda_q2[j],
                                  od_lambda_k2[j], od_subln_g[j], lambda_init, batch, seq)
            h, hb = _out_proj_norm([mix], [od_w_out[j].astype(BF16)], h, od_ln_g[j], od_ln_b[j])
    return h.reshape(batch, seq, d).astype(x.dtype)
```

```python
import functools
import math

import jax
import jax.numpy as jnp
from jax import lax
from jax.experimental import pallas as pl
from jax.experimental.pallas import tpu as pltpu

F32 = jnp.float32
BF16 = jnp.bfloat16

D_MODEL = 1024
DEPTH = 4
ROPE_THETA = 10000.0
CHUNK = 128
RET_SUB = 4
NORM_EPS = 1e-5
HEAD = 128

RET_HEADS = 8
LRU_WIDTH = 1024
LRU_BLOCKS = 8
CONV_WIDTH = 4
LRU_C = 8.0
DIFF_HEADS = 8
DIFF_DV = 2 * HEAD

DEEPNORM_ALPHA = (2.0 * DEPTH) ** 0.25
LOG2E = math.log2(math.e)

VMEM_LIMIT = 56000 * 1024

PROJ_TM = 1024
PROJ_TN = 2048
PROJ_CHUNK = 512
OUT_TM = 1024
OUT_CHUNK = 256
LRU_T = 256
ATT_TQ = 512
ATT_TK = 512
ATT_ONES_ROWS = 16


def _params(sem):
    return pltpu.CompilerParams(dimension_semantics=sem, vmem_limit_bytes=VMEM_LIMIT)


def _proj_kernel(x_ref, w_ref, cos_ref, sin_ref, o_ref, wb_ref, *, n_rope_blocks, half_chunks, tn):
    j = pl.program_id(0)
    chunks = tn // PROJ_CHUNK

    @pl.when(pl.program_id(1) == 0)
    def _():
        wb_ref[...] = w_ref[...].astype(BF16)

    xb = x_ref[...].astype(BF16)

    @pl.when(j < n_rope_blocks)
    def _():
        for c in range(chunks):
            c0 = c * PROJ_CHUNK
            acc = jnp.dot(xb, wb_ref[:, c0:c0 + PROJ_CHUNK], preferred_element_type=F32)
            tab = (j * chunks + c) // half_chunks
            cos = cos_ref[tab]
            sin = sin_ref[tab]
            for h in range(PROJ_CHUNK // HEAD):
                a = acc[:, h * HEAD:(h + 1) * HEAD]
                r = a * cos + pltpu.roll(a, HEAD // 2, 1) * sin
                o_ref[:, c0 + h * HEAD:c0 + (h + 1) * HEAD] = r.astype(o_ref.dtype)

    @pl.when(j >= n_rope_blocks)
    def _():
        for c in range(chunks):
            c0 = c * PROJ_CHUNK
            acc = jnp.dot(xb, wb_ref[:, c0:c0 + PROJ_CHUNK], preferred_element_type=F32)
            o_ref[:, c0:c0 + PROJ_CHUNK] = acc.astype(o_ref.dtype)


def _project(xb, w, layer, cos_tab, sin_tab, rope_cols, seq):
    m, k = xb.shape
    n = w.shape[2]
    tm, tn = min(PROJ_TM, seq), PROJ_TN
    assert rope_cols % tn == 0 and (rope_cols // 2) % PROJ_CHUNK == 0 and seq % tm == 0
    seq_blocks = seq // tm

    def tab_map(j, i):
        return (0, i % seq_blocks, 0)

    return pl.pallas_call(
        functools.partial(_proj_kernel, n_rope_blocks=rope_cols // tn,
                          half_chunks=rope_cols // 2 // PROJ_CHUNK, tn=tn),
        grid=(n // tn, m // tm),
        in_specs=[
            pl.BlockSpec((tm, k), lambda j, i: (i, 0)),
            pl.BlockSpec((None, k, tn), lambda j, i: (layer, 0, j)),
            pl.BlockSpec((2, tm, HEAD), tab_map),
            pl.BlockSpec((2, tm, HEAD), tab_map),
        ],
        out_specs=pl.BlockSpec((tm, tn), lambda j, i: (i, j)),
        out_shape=jax.ShapeDtypeStruct((m, n), BF16),
        scratch_shapes=[pltpu.VMEM((k, tn), BF16)],
        compiler_params=_params(("arbitrary", "arbitrary")),
        name="proj",
    )(xb, w, cos_tab, sin_tab)


def _retention_kernel(q_ref, k_ref, v_ref, g_ref, decay_ref, qdec_ref, kend_ref, cd_ref,
                      gng_ref, gnb_ref, o_ref, state_ref, mix_ref):
    n = pl.program_id(1)
    c = CHUNK

    @pl.when(n == 0)
    def _():
        state_ref[...] = jnp.zeros_like(state_ref)

    blk_r = lax.broadcasted_iota(jnp.int32, (2 * HEAD, 2 * HEAD), 0) // HEAD
    blk_c = lax.broadcasted_iota(jnp.int32, (2 * HEAD, 2 * HEAD), 1) // HEAD
    on_block = blk_r == blk_c

    states = [state_ref[p] for p in range(RET_HEADS // 2)]
    zeros = jnp.zeros((c, HEAD), BF16)

    def mix(rows):
        for p in range(RET_HEADS // 2):
            h0 = slice(2 * p * HEAD, (2 * p + 1) * HEAD)
            h1 = slice((2 * p + 1) * HEAD, (2 * p + 2) * HEAD)
            pair = slice(2 * p * HEAD, (2 * p + 2) * HEAD)
            q_pair = q_ref[rows, pair]
            q_rows = jnp.concatenate([q_ref[rows, h0], q_ref[rows, h1]], axis=0)
            k_rows = jnp.concatenate([k_ref[rows, h0], k_ref[rows, h1]], axis=0)
            s_full = lax.dot_general(q_rows, k_rows, (((1,), (1,)), ((), ())),
                                     preferred_element_type=F32)
            s_pair = jnp.concatenate([s_full[0:c, 0:c] * decay_ref[2 * p],
                                      s_full[c:2 * c, c:2 * c] * decay_ref[2 * p + 1]], axis=1)
            v_pair = v_ref[rows, pair]
            v_bd = jnp.concatenate(
                [jnp.concatenate([v_pair[:, 0:HEAD], zeros], axis=1),
                 jnp.concatenate([zeros, v_pair[:, HEAD:2 * HEAD]], axis=1)], axis=0)
            intra = jnp.dot(s_pair.astype(BF16), v_bd, preferred_element_type=F32)
            state = states[p]
            cross = jnp.dot(q_pair, state.astype(BF16), preferred_element_type=F32)
            v_end = (v_pair.astype(F32) * kend_ref[:, pair]).astype(BF16)
            kv = lax.dot_general(k_ref[rows, pair], v_end, (((0,), (0,)), ((), ())),
                                 preferred_element_type=F32)
            states[p] = state * cd_ref[p] + jnp.where(on_block, kv, 0.0)
            mix_ref[rows, pair] = intra + cross * qdec_ref[:, pair]

    def norm_gate(rows):
        for h in range(RET_HEADS):
            hs = slice(h * HEAD, (h + 1) * HEAD)
            oh = mix_ref[rows, hs]
            mu = jnp.mean(oh, axis=-1, keepdims=True)
            d = oh - mu
            var = jnp.mean(d * d, axis=-1, keepdims=True)
            oh = d * lax.rsqrt(var + NORM_EPS) * gng_ref[:, hs] + gnb_ref[:, hs]
            g = g_ref[rows, hs].astype(F32)
            oh = oh * (g * jax.nn.sigmoid(g))
            o_ref[rows, hs] = oh.astype(o_ref.dtype)

    for sub in range(RET_SUB):
        mix(slice(sub * c, (sub + 1) * c))
        if sub > 0:
            norm_gate(slice((sub - 1) * c, sub * c))
    norm_gate(slice((RET_SUB - 1) * c, RET_SUB * c))
    for p in range(RET_HEADS // 2):
        state_ref[p] = states[p]


def _retention(proj, gn_g, gn_b, batch, seq):
    m = proj.shape[0]
    rows_blk = RET_SUB * CHUNK
    nchunk = seq // rows_blk
    width = RET_HEADS * HEAD
    log_g = jnp.log(1.0 - 2.0 ** (-5.0 - jnp.arange(RET_HEADS, dtype=F32)))
    idx = jnp.arange(CHUNK, dtype=F32)
    rel = idx[:, None] - idx[None, :]
    decay = jnp.where(rel[None] >= 0,
                      jnp.exp(log_g[:, None, None] * jnp.maximum(rel, 0.0)[None]), 0.0)
    kend = jnp.exp(log_g[:, None] * (CHUNK - 1 - idx)[None, :])
    qdec = jnp.exp(log_g[:, None] * (idx + 1.0)[None, :])
    kend = jnp.repeat(kend.T, HEAD, axis=1)
    qdec = jnp.repeat(qdec.T, HEAD, axis=1)
    cd = jnp.exp(log_g * CHUNK)
    npair = RET_HEADS // 2
    blk = jnp.arange(2 * HEAD) // HEAD
    cd_bd = jnp.where(blk[None, :, None] == blk[None, None, :],
                      cd.reshape(npair, 2)[:, blk][:, :, None], 0.0).astype(F32)

    def row(b, n):
        return b * nchunk + n

    def col_spec(c):
        return pl.BlockSpec((rows_blk, width), lambda b, n, c=c: (row(b, n), c))

    def const(shape):
        return pl.BlockSpec(shape, lambda b, n: (0,) * len(shape))

    return pl.pallas_call(
        _retention_kernel,
        grid=(batch, nchunk),
        in_specs=[
            col_spec(0), col_spec(1), col_spec(2), col_spec(3),
            const((RET_HEADS, CHUNK, CHUNK)), const((CHUNK, width)), const((CHUNK, width)),
            const((npair, 2 * HEAD, 2 * HEAD)), const((1, width)), const((1, width)),
        ],
        out_specs=pl.BlockSpec((rows_blk, width), lambda b, n: (row(b, n), 0)),
        out_shape=jax.ShapeDtypeStruct((m, width), BF16),
        scratch_shapes=[pltpu.VMEM((npair, 2 * HEAD, 2 * HEAD), F32),
                        pltpu.VMEM((rows_blk, width), F32)],
        compiler_params=_params(("arbitrary", "arbitrary")),
        name="retention",
    )(proj, proj, proj, proj, decay, qdec, kend, cd_bd,
      gn_g.reshape(1, width), gn_b.reshape(1, width))


def _softplus(z):
    return jnp.maximum(z, 0.0) + jnp.log1p(jnp.exp(-jnp.abs(z)))


def _lru_kernel(x_ref, g_ref, cw_ref, cb_ref, wa_ref, ba_ref, wx_ref, bx_ref, lam_ref,
                o_ref, xpad_ref, a_ref, u_ref, h_ref, carry_ref, *, t_blk):
    t = pl.program_id(1)
    pad = 8

    @pl.when(t == 0)
    def _():
        xpad_ref[0:pad, :] = jnp.zeros((pad, LRU_WIDTH), F32)
        carry_ref[...] = jnp.zeros_like(carry_ref)

    xpad_ref[pad:pad + t_blk, :] = x_ref[...].astype(F32)
    xc = cb_ref[...] + jnp.zeros((t_blk, LRU_WIDTH), F32)
    for w in range(CONV_WIDTH):
        off = pad - (CONV_WIDTH - 1) + w
        xc = xc + cw_ref[w:w + 1, :] * xpad_ref[off:off + t_blk, :]
    xpad_ref[0:pad, :] = xpad_ref[t_blk:t_blk + pad, :]

    neg_c_sp = -LRU_C * _softplus(-lam_ref[...])
    bd = LRU_WIDTH // LRU_BLOCKS
    for blk in range(LRU_BLOCKS):
        cs = slice(blk * bd, (blk + 1) * bd)
        xg = xc[:, cs]
        xgb = xg.astype(BF16)
        r = jax.nn.sigmoid(jnp.dot(xgb, wa_ref[blk], preferred_element_type=F32) + ba_ref[:, cs])
        i = jax.nn.sigmoid(jnp.dot(xgb, wx_ref[blk], preferred_element_type=F32) + bx_ref[:, cs])
        log_a = r * neg_c_sp[:, cs]
        a = jnp.exp(log_a)
        a_ref[:, cs] = a
        u_ref[:, cs] = xg * i * jnp.sqrt(-jnp.tanh(log_a) * (a * a + 1.0))

    def body(i, h):
        r0 = pl.multiple_of(i * 8, 8)
        a8 = a_ref[pl.ds(r0, 8), :]
        u8 = u_ref[pl.ds(r0, 8), :]
        rows = []
        for j in range(8):
            h = a8[j:j + 1, :] * h + u8[j:j + 1, :]
            rows.append(h)
        h_ref[pl.ds(r0, 8), :] = jnp.concatenate(rows, axis=0)
        return h

    carry_ref[...] = lax.fori_loop(0, t_blk // 8, body, carry_ref[...])
    g = g_ref[...].astype(F32)
    o_ref[...] = (h_ref[...] * (g * jax.nn.sigmoid(g))).astype(o_ref.dtype)


def _rg_lru(proj, conv_w, conv_b, wa, ba, wx, bx, lam, batch, seq):
    m = proj.shape[0]
    t_blk = LRU_T
    nblk = seq // t_blk
    w = LRU_WIDTH
    bd = w // LRU_BLOCKS

    def col_spec(c):
        return pl.BlockSpec((t_blk, w), lambda b, t, c=c: (b * nblk + t, c))

    vec = pl.BlockSpec((1, w), lambda b, t: (0, 0))
    gate_w = pl.BlockSpec((LRU_BLOCKS, bd, bd), lambda b, t: (0, 0, 0))
    return pl.pallas_call(
        functools.partial(_lru_kernel, t_blk=t_blk),
        grid=(batch, nblk),
        in_specs=[
            col_spec(4), col_spec(5),
            pl.BlockSpec((CONV_WIDTH, w), lambda b, t: (0, 0)), vec,
            gate_w, vec, gate_w, vec, vec,
        ],
        out_specs=pl.BlockSpec((t_blk, w), lambda b, t: (b * nblk + t, 0)),
        out_shape=jax.ShapeDtypeStruct((m, w), BF16),
        scratch_shapes=[
            pltpu.VMEM((t_blk + 8, w), F32),
            pltpu.VMEM((t_blk, w), F32),
            pltpu.VMEM((t_blk, w), F32),
            pltpu.VMEM((t_blk, w), F32),
            pltpu.VMEM((1, w), F32),
        ],
        compiler_params=_params(("arbitrary", "arbitrary")),
        name="rg_lru",
    )(proj, proj, conv_w, conv_b.reshape(1, w), wa.astype(BF16), ba.reshape(1, w),
      wx.astype(BF16), bx.reshape(1, w), lam.reshape(1, w))


def _attn_kernel(q_ref, k_ref, v_ref, g_ref, bias_ref, lq1_ref, lk1_ref, lq2_ref, lk2_ref,
                 sg_ref, o_ref, vt_ref, acc_ref, s_ref, *, seq, lambda_init):
    tq = ATT_TQ
    tk = ATT_TK
    for c in range(seq // tk):
        vt_ref[c, 0:DIFF_DV, :] = v_ref[c * tk:(c + 1) * tk, :].T
        vt_ref[c, DIFF_DV:, :] = jnp.ones((ATT_ONES_ROWS, tk), BF16)

    lam = (jnp.exp(jnp.sum(lq1_ref[...] * lk1_ref[...], axis=-1, keepdims=True))
           - jnp.exp(jnp.sum(lq2_ref[...] * lk2_ref[...], axis=-1, keepdims=True))
           + lambda_init)

    half = tk // 2
    steps = []
    for qi in range(seq // tq):
        for j in range(qi):
            steps.append((qi, j * tk, tk, 0, False, j == 0, False))
        steps.append((qi, qi * tk, half, 0, True, qi == 0, False))
        steps.append((qi, qi * tk + half, half, half, True, False, True))

    def issue_scores(st, slot):
        qi, k0, nk, qlo, _, _, _ = st
        qblk = q_ref[qi * tq + qlo:(qi + 1) * tq, :]
        kblk = k_ref[k0:k0 + nk, :]
        for c in range(2):
            s_ref[slot, c, 0:nk, 0:tq - qlo] = lax.dot_general(
                kblk[:, c * HEAD:(c + 1) * HEAD], qblk[:, c * HEAD:(c + 1) * HEAD],
                (((1,), (1,)), ((), ())), preferred_element_type=F32)

    def consume(st, slot, m):
        qi, k0, nk, qlo, masked, first, _ = st
        nq = tq - qlo
        for c in range(2):
            s = s_ref[slot, c, 0:nk, 0:nq]
            if masked:
                s = s + bias_ref[:, 0:nq]
            mx = jnp.max(s, axis=0, keepdims=True)
            m_old = None if first else m[c][:, qlo:]
            m_new = mx if first else jnp.maximum(m_old, mx)
            p = jnp.exp2((s - m_new).astype(BF16))
            vt = vt_ref[k0 // tk, :, k0 % tk:k0 % tk + nk]
            pv = jnp.dot(vt, p, preferred_element_type=F32)
            if first:
                acc_ref[c] = pv
            else:
                acc_ref[c, :, qlo:] = jnp.exp2(m_old - m_new) * acc_ref[c, :, qlo:] + pv
            m[c] = m_new if qlo == 0 else jnp.concatenate([m[c][:, :qlo], m_new], axis=1)

    issue_scores(steps[0], 0)
    m = [None, None]
    for i, st in enumerate(steps):
        if i + 1 < len(steps):
            issue_scores(steps[i + 1], (i + 1) % 2)
        consume(st, i % 2, m)
        if not st[6]:
            continue
        qi = st[0]
        dv = DIFF_DV
        inv_l = [1.0 / acc_ref[c, dv:dv + 1, :] for c in range(2)]
        o_t = acc_ref[0, 0:dv, :] * inv_l[0] - lam * (acc_ref[1, 0:dv, :] * inv_l[1])
        ms = jnp.mean(o_t * o_t, axis=0, keepdims=True)
        o = (o_t * lax.rsqrt(ms + NORM_EPS)).T
        g = g_ref[qi * tq:(qi + 1) * tq, :].astype(F32)
        o = (o * sg_ref[...]) * (1.0 - lambda_init) * (g * jax.nn.sigmoid(g))
        o_ref[qi * tq:(qi + 1) * tq, :] = o.astype(o_ref.dtype)


def _diff_attention(proj, lq1, lk1, lq2, lk2, subln_g, lambda_init, batch, seq):
    m = proj.shape[0]
    tq, tk = ATT_TQ, ATT_TK
    assert tq == tk and seq % tk == 0
    wblk = DIFF_DV
    per_region = DIFF_HEADS
    kk = jnp.arange(tk // 2, dtype=jnp.int32)[:, None]
    qq = jnp.arange(tq, dtype=jnp.int32)[None, :]
    bias = jnp.where(kk <= qq, 0.0, -jnp.inf).astype(F32)

    def blk(region):
        return pl.BlockSpec((seq, wblk), lambda b, h, r=region: (b, r * per_region + h))

    vec = pl.BlockSpec((1, HEAD), lambda b, h: (0, 0))
    return pl.pallas_call(
        functools.partial(_attn_kernel, seq=seq, lambda_init=lambda_init),
        grid=(batch, DIFF_HEADS),
        in_specs=[
            blk(0), blk(1), blk(2), blk(3),
            pl.BlockSpec((tk // 2, tq), lambda b, h: (0, 0)),
            vec, vec, vec, vec,
            pl.BlockSpec((1, wblk), lambda b, h: (0, 0)),
        ],
        out_specs=pl.BlockSpec((seq, wblk), lambda b, h: (b, h)),
        out_shape=jax.ShapeDtypeStruct((m, DIFF_HEADS * wblk), BF16),
        scratch_shapes=[
            pltpu.VMEM((seq // tk, wblk + ATT_ONES_ROWS, tk), BF16),
            pltpu.VMEM((2, wblk + ATT_ONES_ROWS, tq), F32),
            pltpu.VMEM((2, 2, tk, tq), F32),
        ],
        compiler_params=_params(("arbitrary", "arbitrary")),
        name="diff_attn",
    )(proj, proj, proj, proj, bias, lq1.reshape(1, HEAD), lk1.reshape(1, HEAD),
      lq2.reshape(1, HEAD), lk2.reshape(1, HEAD), subln_g.reshape(1, wblk))


def _out_kernel(*refs, n_in):
    acts = refs[:n_in]
    w_ref, h_ref, g_ref, b_ref, o_ref, ob_ref, wb_ref = refs[n_in:]

    @pl.when(pl.program_id(0) == 0)
    def _():
        wb_ref[...] = w_ref[...].astype(BF16)

    for r in range(h_ref.shape[0] // OUT_CHUNK):
        rows = slice(r * OUT_CHUNK, (r + 1) * OUT_CHUNK)
        y = None
        k0 = 0
        for a in acts:
            kw = a.shape[1]
            part = jnp.dot(a[rows, :], wb_ref[k0:k0 + kw, :], preferred_element_type=F32)
            y = part if y is None else y + part
            k0 += kw
        z = DEEPNORM_ALPHA * h_ref[rows, :] + y
        mu = jnp.mean(z, axis=-1, keepdims=True)
        d = z - mu
        var = jnp.mean(d * d, axis=-1, keepdims=True)
        out = d * lax.rsqrt(var + NORM_EPS) * g_ref[...] + b_ref[...]
        o_ref[rows, :] = out
        ob_ref[rows, :] = out.astype(BF16)


def _out_proj_norm(acts, w, layer, h, ln_g, ln_b):
    m, d = h.shape
    tm = OUT_TM
    n_in = len(acts)
    wshape = w.shape[1:]
    assert sum(a.shape[1] for a in acts) == wshape[0]
    in_specs = [pl.BlockSpec((tm, a.shape[1]), lambda i: (i, 0)) for a in acts]
    in_specs += [pl.BlockSpec((None,) + wshape, lambda i: (layer, 0, 0)),
                 pl.BlockSpec((tm, d), lambda i: (i, 0)),
                 pl.BlockSpec((1, d), lambda i: (0, 0)),
                 pl.BlockSpec((1, d), lambda i: (0, 0))]
    return pl.pallas_call(
        functools.partial(_out_kernel, n_in=n_in),
        grid=(m // tm,),
        in_specs=in_specs,
        out_specs=[pl.BlockSpec((tm, d), lambda i: (i, 0)),
                   pl.BlockSpec((tm, d), lambda i: (i, 0))],
        out_shape=[jax.ShapeDtypeStruct((m, d), F32), jax.ShapeDtypeStruct((m, d), BF16)],
        scratch_shapes=[pltpu.VMEM(wshape, BF16)],
        compiler_params=_params(("arbitrary",)),
        name="out_norm",
    )(*acts, w, h, ln_g.reshape(1, d), ln_b.reshape(1, d))


def _rope_tables(seq, scale_first, scale_second):
    half = HEAD // 2
    inv_freq = ROPE_THETA ** (-jnp.arange(half, dtype=F32) / half)
    ang = jnp.arange(seq, dtype=F32)[:, None] * inv_freq[None, :]
    cos = jnp.cos(ang)
    sin = jnp.sin(ang)
    cos_full = jnp.concatenate([cos, cos], axis=-1)
    sin_full = jnp.concatenate([-sin, sin], axis=-1)
    scales = jnp.array([scale_first, scale_second], F32)[:, None, None]
    return cos_full[None] * scales, sin_full[None] * scales


def kernel(x, ev_w_in, ev_conv_w, ev_conv_b, ev_gate_a_w, ev_gate_a_b, ev_gate_x_w, ev_gate_x_b,
           ev_lru_lambda, ev_ret_gn_g, ev_ret_gn_b, ev_w_out, ev_ln_g, ev_ln_b,
           od_w_in, od_lambda_q1, od_lambda_k1, od_lambda_q2, od_lambda_k2, od_subln_g, od_w_out,
           od_ln_g, od_ln_b):
    batch, seq, d = x.shape
    m = batch * seq
    ret_width = RET_HEADS * HEAD
    ev_cos, ev_sin = _rope_tables(seq, 1.0, HEAD ** -0.5)
    od_cos, od_sin = _rope_tables(seq, HEAD ** -0.5 * LOG2E, 1.0)

    h = x.reshape(m, d).astype(F32)
    hb = h
    for l in range(DEPTH):
        j = l // 2
        if l % 2 == 0:
            proj = _project(hb, ev_w_in, j, ev_cos, ev_sin, 2 * ret_width, seq)
            ret = _retention(proj, ev_ret_gn_g[j], ev_ret_gn_b[j], batch, seq)
            lru = _rg_lru(proj, ev_conv_w[j], ev_conv_b[j], ev_gate_a_w[j], ev_gate_a_b[j],
                          ev_gate_x_w[j], ev_gate_x_b[j], ev_lru_lambda[j], batch, seq)
            h, hb = _out_proj_norm([ret, lru], ev_w_out, j, h, ev_ln_g[j], ev_ln_b[j])
        else:
            lambda_init = 0.8 - 0.6 * math.exp(-0.3 * l)
            proj = _project(hb, od_w_in, j, od_cos, od_sin, 2 * 2 * DIFF_HEADS * HEAD, seq)
            mix = _diff_attention(proj, od_lambda_q1[j], od_lambda_k1[j], od_lambda_q2[j],
                                  od_lambda_k2[j], od_subln_g[j], lambda_init, batch, seq)
            h, hb = _out_proj_norm([mix], od_w_out, j, h, od_ln_g[j], od_ln_b[j])
    return h.reshape(batch, seq, d).astype(x.dtype)
```

```python
import functools
import math

import jax
import jax.numpy as jnp
from jax import lax
from jax.experimental import pallas as pl
from jax.experimental.pallas import tpu as pltpu

F32 = jnp.float32
BF16 = jnp.bfloat16

D_MODEL = 1024
DEPTH = 4
ROPE_THETA = 10000.0
CHUNK = 128
RET_SUB = 4
NORM_EPS = 1e-5
HEAD = 128

RET_HEADS = 8
LRU_WIDTH = 1024
LRU_BLOCKS = 8
CONV_WIDTH = 4
LRU_C = 8.0
DIFF_HEADS = 8
DIFF_DV = 2 * HEAD

DEEPNORM_ALPHA = (2.0 * DEPTH) ** 0.25
LOG2E = math.log2(math.e)

VMEM_LIMIT = 56000 * 1024

PROJ_TM = 1024
PROJ_TN = 2048
PROJ_CHUNK = 512
OUT_TM = 1024
OUT_CHUNK = 256
LRU_T = 128
LRU_NB = 8
LRU_SCAN_UNROLL = 8
ATT_TQ = 512
ATT_TK = 512
ATT_ONES_ROWS = 16


def _params(sem):
    return pltpu.CompilerParams(dimension_semantics=sem, vmem_limit_bytes=VMEM_LIMIT)


def _sigmoid(x):
    return 0.5 * jnp.tanh(0.5 * x) + 0.5


def _sqrt_nonneg(y):
    return jnp.exp2(0.5 * jnp.log2(y))


def _proj_kernel(x_ref, w_ref, cos_ref, sin_ref, o_ref, wb_ref, *, n_rope_blocks, half_chunks, tn):
    j = pl.program_id(0)
    chunks = tn // PROJ_CHUNK

    @pl.when(pl.program_id(1) == 0)
    def _():
        wb_ref[...] = w_ref[...].astype(BF16)

    xb = x_ref[...].astype(BF16)

    @pl.when(j < n_rope_blocks)
    def _():
        for c in range(chunks):
            c0 = c * PROJ_CHUNK
            acc = jnp.dot(xb, wb_ref[:, c0:c0 + PROJ_CHUNK], preferred_element_type=F32)
            tab = (j * chunks + c) // half_chunks
            cos = cos_ref[tab]
            sin = sin_ref[tab]
            for h in range(PROJ_CHUNK // HEAD):
                a = acc[:, h * HEAD:(h + 1) * HEAD]
                r = a * cos + pltpu.roll(a, HEAD // 2, 1) * sin
                o_ref[:, c0 + h * HEAD:c0 + (h + 1) * HEAD] = r.astype(o_ref.dtype)

    @pl.when(j >= n_rope_blocks)
    def _():
        for c in range(chunks):
            c0 = c * PROJ_CHUNK
            acc = jnp.dot(xb, wb_ref[:, c0:c0 + PROJ_CHUNK], preferred_element_type=F32)
            o_ref[:, c0:c0 + PROJ_CHUNK] = acc.astype(o_ref.dtype)


def _project(xb, w, layer, cos_tab, sin_tab, rope_cols, seq):
    m, k = xb.shape
    n = w.shape[2]
    tm, tn = min(PROJ_TM, seq), PROJ_TN
    assert rope_cols % tn == 0 and (rope_cols // 2) % PROJ_CHUNK == 0 and seq % tm == 0
    seq_blocks = seq // tm

    def tab_map(j, i):
        return (0, i % seq_blocks, 0)

    return pl.pallas_call(
        functools.partial(_proj_kernel, n_rope_blocks=rope_cols // tn,
                          half_chunks=rope_cols // 2 // PROJ_CHUNK, tn=tn),
        grid=(n // tn, m // tm),
        in_specs=[
            pl.BlockSpec((tm, k), lambda j, i: (i, 0)),
            pl.BlockSpec((None, k, tn), lambda j, i: (layer, 0, j)),
            pl.BlockSpec((2, tm, HEAD), tab_map),
            pl.BlockSpec((2, tm, HEAD), tab_map),
        ],
        out_specs=pl.BlockSpec((tm, tn), lambda j, i: (i, j)),
        out_shape=jax.ShapeDtypeStruct((m, n), BF16),
        scratch_shapes=[pltpu.VMEM((k, tn), BF16)],
        compiler_params=_params(("arbitrary", "arbitrary")),
        name="proj",
    )(xb, w, cos_tab, sin_tab)


def _retention_kernel(q_ref, k_ref, v_ref, g_ref, decay_ref, qdec_ref, kend_ref, cd_ref,
                      gng_ref, gnb_ref, o_ref, state_ref, mix_ref):
    n = pl.program_id(1)
    c = CHUNK

    @pl.when(n == 0)
    def _():
        state_ref[...] = jnp.zeros_like(state_ref)

    blk_r = lax.broadcasted_iota(jnp.int32, (2 * HEAD, 2 * HEAD), 0) // HEAD
    blk_c = lax.broadcasted_iota(jnp.int32, (2 * HEAD, 2 * HEAD), 1) // HEAD
    on_block = blk_r == blk_c

    states = [state_ref[p] for p in range(RET_HEADS // 2)]
    zeros = jnp.zeros((c, HEAD), BF16)

    def mix(rows):
        for p in range(RET_HEADS // 2):
            h0 = slice(2 * p * HEAD, (2 * p + 1) * HEAD)
            h1 = slice((2 * p + 1) * HEAD, (2 * p + 2) * HEAD)
            pair = slice(2 * p * HEAD, (2 * p + 2) * HEAD)
            q_pair = q_ref[rows, pair]
            q_rows = jnp.concatenate([q_ref[rows, h0], q_ref[rows, h1]], axis=0)
            k_rows = jnp.concatenate([k_ref[rows, h0], k_ref[rows, h1]], axis=0)
            s_full = lax.dot_general(q_rows, k_rows, (((1,), (1,)), ((), ())),
                                     preferred_element_type=F32)
            s_pair = jnp.concatenate([s_full[0:c, 0:c] * decay_ref[2 * p],
                                      s_full[c:2 * c, c:2 * c] * decay_ref[2 * p + 1]], axis=1)
            v_pair = v_ref[rows, pair]
            v_bd = jnp.concatenate(
                [jnp.concatenate([v_pair[:, 0:HEAD], zeros], axis=1),
                 jnp.concatenate([zeros, v_pair[:, HEAD:2 * HEAD]], axis=1)], axis=0)
            intra = jnp.dot(s_pair.astype(BF16), v_bd, preferred_element_type=F32)
            state = states[p]
            cross = jnp.dot(q_pair, state.astype(BF16), preferred_element_type=F32)
            v_end = (v_pair.astype(F32) * kend_ref[:, pair]).astype(BF16)
            kv = lax.dot_general(k_ref[rows, pair], v_end, (((0,), (0,)), ((), ())),
                                 preferred_element_type=F32)
            states[p] = state * cd_ref[p] + jnp.where(on_block, kv, 0.0)
            mix_ref[rows, pair] = intra + cross * qdec_ref[:, pair]

    def norm_gate(rows):
        for h in range(RET_HEADS):
            hs = slice(h * HEAD, (h + 1) * HEAD)
            oh = mix_ref[rows, hs]
            mu = jnp.mean(oh, axis=-1, keepdims=True)
            d = oh - mu
            var = jnp.mean(d * d, axis=-1, keepdims=True)
            oh = d * lax.rsqrt(var + NORM_EPS) * gng_ref[:, hs] + gnb_ref[:, hs]
            g = g_ref[rows, hs].astype(F32)
            oh = oh * (g * _sigmoid(g))
            o_ref[rows, hs] = oh.astype(o_ref.dtype)

    for sub in range(RET_SUB):
        mix(slice(sub * c, (sub + 1) * c))
        if sub > 0:
            norm_gate(slice((sub - 1) * c, sub * c))
    norm_gate(slice((RET_SUB - 1) * c, RET_SUB * c))
    for p in range(RET_HEADS // 2):
        state_ref[p] = states[p]


def _retention(proj, gn_g, gn_b, batch, seq):
    m = proj.shape[0]
    rows_blk = RET_SUB * CHUNK
    nchunk = seq // rows_blk
    width = RET_HEADS * HEAD
    log_g = jnp.log(1.0 - 2.0 ** (-5.0 - jnp.arange(RET_HEADS, dtype=F32)))
    idx = jnp.arange(CHUNK, dtype=F32)
    rel = idx[:, None] - idx[None, :]
    decay = jnp.where(rel[None] >= 0,
                      jnp.exp(log_g[:, None, None] * jnp.maximum(rel, 0.0)[None]), 0.0)
    kend = jnp.exp(log_g[:, None] * (CHUNK - 1 - idx)[None, :])
    qdec = jnp.exp(log_g[:, None] * (idx + 1.0)[None, :])
    kend = jnp.repeat(kend.T, HEAD, axis=1)
    qdec = jnp.repeat(qdec.T, HEAD, axis=1)
    cd = jnp.exp(log_g * CHUNK)
    npair = RET_HEADS // 2
    blk = jnp.arange(2 * HEAD) // HEAD
    cd_bd = jnp.where(blk[None, :, None] == blk[None, None, :],
                      cd.reshape(npair, 2)[:, blk][:, :, None], 0.0).astype(F32)

    def row(b, n):
        return b * nchunk + n

    def col_spec(c):
        return pl.BlockSpec((rows_blk, width), lambda b, n, c=c: (row(b, n), c))

    def const(shape):
        return pl.BlockSpec(shape, lambda b, n: (0,) * len(shape))

    return pl.pallas_call(
        _retention_kernel,
        grid=(batch, nchunk),
        in_specs=[
            col_spec(0), col_spec(1), col_spec(2), col_spec(3),
            const((RET_HEADS, CHUNK, CHUNK)), const((CHUNK, width)), const((CHUNK, width)),
            const((npair, 2 * HEAD, 2 * HEAD)), const((1, width)), const((1, width)),
        ],
        out_specs=pl.BlockSpec((rows_blk, width), lambda b, n: (row(b, n), 0)),
        out_shape=jax.ShapeDtypeStruct((m, width), BF16),
        scratch_shapes=[pltpu.VMEM((npair, 2 * HEAD, 2 * HEAD), F32),
                        pltpu.VMEM((rows_blk, width), F32)],
        compiler_params=_params(("arbitrary", "arbitrary")),
        name="retention",
    )(proj, proj, proj, proj, decay, qdec, kend, cd_bd,
      gn_g.reshape(1, width), gn_b.reshape(1, width))


def _softplus(z):
    return jnp.maximum(z, 0.0) + jnp.log1p(jnp.exp(-jnp.abs(z)))


def _lru_kernel(x_ref, g_ref, cw_ref, cb_ref, wa_ref, ba_ref, wx_ref, bx_ref, lam_ref,
                o_ref, xs_ref, hist_ref, a_ref, u_ref, h_ref, carry_ref, *, t_blk):
    t = pl.program_id(1)
    nb = LRU_NB
    bd = LRU_WIDTH // LRU_BLOCKS
    rows = t_blk * nb
    hist = (CONV_WIDTH - 1) * nb

    @pl.when(t == 0)
    def _():
        hist_ref[...] = jnp.zeros_like(hist_ref)
        carry_ref[...] = jnp.zeros_like(carry_ref)

    for b in range(nb):
        xb = x_ref[b].astype(F32)
        for blk in range(LRU_BLOCKS):
            xs_ref[blk, pl.ds(hist + b, t_blk, stride=nb), :] = xb[:, blk * bd:(blk + 1) * bd]

    neg_c_sp = -LRU_C * _softplus(-lam_ref[...])
    for blk in range(LRU_BLOCKS):
        cs = slice(blk * bd, (blk + 1) * bd)
        xs_ref[blk, 0:hist, :] = hist_ref[blk]
        xc = cb_ref[:, cs] + jnp.zeros((rows, bd), F32)
        for w in range(CONV_WIDTH):
            xc = xc + cw_ref[w:w + 1, cs] * xs_ref[blk, w * nb:w * nb + rows, :]
        hist_ref[blk] = xs_ref[blk, rows:rows + hist, :]
        xgb = xc.astype(BF16)
        r = _sigmoid(jnp.dot(xgb, wa_ref[blk], preferred_element_type=F32) + ba_ref[:, cs])
        i = _sigmoid(jnp.dot(xgb, wx_ref[blk], preferred_element_type=F32) + bx_ref[:, cs])
        log_a = r * neg_c_sp[:, cs]
        a = jnp.exp(log_a)
        a_ref[blk] = a
        u_ref[blk] = xc * i * _sqrt_nonneg(-jnp.tanh(log_a) * (a * a + 1.0))

    def body(tt, hs):
        r0 = pl.multiple_of(tt * nb, nb)
        out = []
        for blk in range(LRU_BLOCKS):
            h = a_ref[blk, pl.ds(r0, nb), :] * hs[blk] + u_ref[blk, pl.ds(r0, nb), :]
            h_ref[blk, pl.ds(r0, nb), :] = h
            out.append(h)
        return tuple(out)

    hs = lax.fori_loop(0, t_blk, body, tuple(carry_ref[blk] for blk in range(LRU_BLOCKS)),
                       unroll=LRU_SCAN_UNROLL)
    for blk in range(LRU_BLOCKS):
        carry_ref[blk] = hs[blk]

    for b in range(nb):
        h = jnp.concatenate([h_ref[blk, pl.ds(b, t_blk, stride=nb), :]
                             for blk in range(LRU_BLOCKS)], axis=1)
        g = g_ref[b].astype(F32)
        o_ref[b] = (h * (g * _sigmoid(g))).astype(o_ref.dtype)


def _rg_lru(proj, conv_w, conv_b, wa, ba, wx, bx, lam, batch, seq):
    m, n = proj.shape
    t_blk = min(LRU_T, seq)
    nb = LRU_NB
    assert batch % nb == 0 and seq % t_blk == 0
    w = LRU_WIDTH
    bd = w // LRU_BLOCKS
    proj3 = proj.reshape(batch, seq, n)

    def col_spec(c):
        return pl.BlockSpec((nb, t_blk, w), lambda b, t, c=c: (b, t, c))

    vec = pl.BlockSpec((1, w), lambda b, t: (0, 0))
    gate_w = pl.BlockSpec((LRU_BLOCKS, bd, bd), lambda b, t: (0, 0, 0))
    out = pl.pallas_call(
        functools.partial(_lru_kernel, t_blk=t_blk),
        grid=(batch // nb, seq // t_blk),
        in_specs=[
            col_spec(4), col_spec(5),
            pl.BlockSpec((CONV_WIDTH, w), lambda b, t: (0, 0)), vec,
            gate_w, vec, gate_w, vec, vec,
        ],
        out_specs=pl.BlockSpec((nb, t_blk, w), lambda b, t: (b, t, 0)),
        out_shape=jax.ShapeDtypeStruct((batch, seq, w), BF16),
        scratch_shapes=[
            pltpu.VMEM((LRU_BLOCKS, (t_blk + CONV_WIDTH - 1) * nb, bd), F32),
            pltpu.VMEM((LRU_BLOCKS, (CONV_WIDTH - 1) * nb, bd), F32),
            pltpu.VMEM((LRU_BLOCKS, t_blk * nb, bd), F32),
            pltpu.VMEM((LRU_BLOCKS, t_blk * nb, bd), F32),
            pltpu.VMEM((LRU_BLOCKS, t_blk * nb, bd), F32),
            pltpu.VMEM((LRU_BLOCKS, nb, bd), F32),
        ],
        compiler_params=_params(("arbitrary", "arbitrary")),
        name="rg_lru",
    )(proj3, proj3, conv_w, conv_b.reshape(1, w), wa.astype(BF16), ba.reshape(1, w),
      wx.astype(BF16), bx.reshape(1, w), lam.reshape(1, w))
    return out.reshape(m, w)


def _attn_kernel(q_ref, k_ref, v_ref, g_ref, bias_ref, lq1_ref, lk1_ref, lq2_ref, lk2_ref,
                 sg_ref, o_ref, vt_ref, acc_ref, s_ref, *, seq, lambda_init):
    tq = ATT_TQ
    tk = ATT_TK
    for c in range(seq // tk):
        vt_ref[c, 0:DIFF_DV, :] = v_ref[c * tk:(c + 1) * tk, :].T
        vt_ref[c, DIFF_DV:, :] = jnp.ones((ATT_ONES_ROWS, tk), BF16)

    lam = (jnp.exp(jnp.sum(lq1_ref[...] * lk1_ref[...], axis=-1, keepdims=True))
           - jnp.exp(jnp.sum(lq2_ref[...] * lk2_ref[...], axis=-1, keepdims=True))
           + lambda_init)

    half = tk // 2
    steps = []
    for qi in range(seq // tq):
        for j in range(qi):
            steps.append((qi, j * tk, tk, 0, False, j == 0, False))
        steps.append((qi, qi * tk, half, 0, True, qi == 0, False))
        steps.append((qi, qi * tk + half, half, half, True, False, True))

    def issue_scores(st, slot):
        qi, k0, nk, qlo, _, _, _ = st
        qblk = q_ref[qi * tq + qlo:(qi + 1) * tq, :]
        kblk = k_ref[k0:k0 + nk, :]
        for c in range(2):
            s_ref[slot, c, 0:nk, 0:tq - qlo] = lax.dot_general(
                kblk[:, c * HEAD:(c + 1) * HEAD], qblk[:, c * HEAD:(c + 1) * HEAD],
                (((1,), (1,)), ((), ())), preferred_element_type=F32)

    def consume(st, slot, m):
        qi, k0, nk, qlo, masked, first, _ = st
        nq = tq - qlo
        for c in range(2):
            s = s_ref[slot, c, 0:nk, 0:nq]
            if masked:
                s = s + bias_ref[:, 0:nq]
            mx = jnp.max(s, axis=0, keepdims=True)
            m_old = None if first else m[c][:, qlo:]
            m_new = mx if first else jnp.maximum(m_old, mx)
            p = jnp.exp2((s - m_new).astype(BF16))
            vt = vt_ref[k0 // tk, :, k0 % tk:k0 % tk + nk]
            pv = jnp.dot(vt, p, preferred_element_type=F32)
            if first:
                acc_ref[c] = pv
            else:
                acc_ref[c, :, qlo:] = jnp.exp2(m_old - m_new) * acc_ref[c, :, qlo:] + pv
            m[c] = m_new if qlo == 0 else jnp.concatenate([m[c][:, :qlo], m_new], axis=1)

    issue_scores(steps[0], 0)
    m = [None, None]
    for i, st in enumerate(steps):
        if i + 1 < len(steps):
            issue_scores(steps[i + 1], (i + 1) % 2)
        consume(st, i % 2, m)
        if not st[6]:
            continue
        qi = st[0]
        dv = DIFF_DV
        inv_l = [1.0 / acc_ref[c, dv:dv + 1, :] for c in range(2)]
        o_t = acc_ref[0, 0:dv, :] * inv_l[0] - lam * (acc_ref[1, 0:dv, :] * inv_l[1])
        ms = jnp.mean(o_t * o_t, axis=0, keepdims=True)
        o = (o_t * lax.rsqrt(ms + NORM_EPS)).T
        g = g_ref[qi * tq:(qi + 1) * tq, :].astype(F32)
        o = (o * sg_ref[...]) * (1.0 - lambda_init) * (g * _sigmoid(g))
        o_ref[qi * tq:(qi + 1) * tq, :] = o.astype(o_ref.dtype)


def _diff_attention(proj, lq1, lk1, lq2, lk2, subln_g, lambda_init, batch, seq):
    m = proj.shape[0]
    tq, tk = ATT_TQ, ATT_TK
    assert tq == tk and seq % tk == 0
    wblk = DIFF_DV
    per_region = DIFF_HEADS
    kk = jnp.arange(tk // 2, dtype=jnp.int32)[:, None]
    qq = jnp.arange(tq, dtype=jnp.int32)[None, :]
    bias = jnp.where(kk <= qq, 0.0, -jnp.inf).astype(F32)

    def blk(region):
        return pl.BlockSpec((seq, wblk), lambda b, h, r=region: (b, r * per_region + h))

    vec = pl.BlockSpec((1, HEAD), lambda b, h: (0, 0))
    return pl.pallas_call(
        functools.partial(_attn_kernel, seq=seq, lambda_init=lambda_init),
        grid=(batch, DIFF_HEADS),
        in_specs=[
            blk(0), blk(1), blk(2), blk(3),
            pl.BlockSpec((tk // 2, tq), lambda b, h: (0, 0)),
            vec, vec, vec, vec,
            pl.BlockSpec((1, wblk), lambda b, h: (0, 0)),
        ],
        out_specs=pl.BlockSpec((seq, wblk), lambda b, h: (b, h)),
        out_shape=jax.ShapeDtypeStruct((m, DIFF_HEADS * wblk), BF16),
        scratch_shapes=[
            pltpu.VMEM((seq // tk, wblk + ATT_ONES_ROWS, tk), BF16),
            pltpu.VMEM((2, wblk + ATT_ONES_ROWS, tq), F32),
            pltpu.VMEM((2, 2, tk, tq), F32),
        ],
        compiler_params=_params(("arbitrary", "arbitrary")),
        name="diff_attn",
    )(proj, proj, proj, proj, bias, lq1.reshape(1, HEAD), lk1.reshape(1, HEAD),
      lq2.reshape(1, HEAD), lk2.reshape(1, HEAD), subln_g.reshape(1, wblk))


def _out_kernel(*refs, n_in):
    acts = refs[:n_in]
    w_ref, h_ref, g_ref, b_ref, o_ref, ob_ref, wb_ref = refs[n_in:]

    @pl.when(pl.program_id(0) == 0)
    def _():
        wb_ref[...] = w_ref[...].astype(BF16)

    for r in range(h_ref.shape[0] // OUT_CHUNK):
        rows = slice(r * OUT_CHUNK, (r + 1) * OUT_CHUNK)
        y = None
        k0 = 0
        for a in acts:
            kw = a.shape[1]
            part = jnp.dot(a[rows, :], wb_ref[k0:k0 + kw, :], preferred_element_type=F32)
            y = part if y is None else y + part
            k0 += kw
        z = DEEPNORM_ALPHA * h_ref[rows, :] + y
        mu = jnp.mean(z, axis=-1, keepdims=True)
        d = z - mu
        var = jnp.mean(d * d, axis=-1, keepdims=True)
        out = d * lax.rsqrt(var + NORM_EPS) * g_ref[...] + b_ref[...]
        o_ref[rows, :] = out
        ob_ref[rows, :] = out.astype(BF16)


def _out_proj_norm(acts, w, layer, h, ln_g, ln_b):
    m, d = h.shape
    tm = OUT_TM
    n_in = len(acts)
    wshape = w.shape[1:]
    assert sum(a.shape[1] for a in acts) == wshape[0]
    in_specs = [pl.BlockSpec((tm, a.shape[1]), lambda i: (i, 0)) for a in acts]
    in_specs += [pl.BlockSpec((None,) + wshape, lambda i: (layer, 0, 0)),
                 pl.BlockSpec((tm, d), lambda i: (i, 0)),
                 pl.BlockSpec((1, d), lambda i: (0, 0)),
                 pl.BlockSpec((1, d), lambda i: (0, 0))]
    return pl.pallas_call(
        functools.partial(_out_kernel, n_in=n_in),
        grid=(m // tm,),
        in_specs=in_specs,
        out_specs=[pl.BlockSpec((tm, d), lambda i: (i, 0)),
                   pl.BlockSpec((tm, d), lambda i: (i, 0))],
        out_shape=[jax.ShapeDtypeStruct((m, d), F32), jax.ShapeDtypeStruct((m, d), BF16)],
        scratch_shapes=[pltpu.VMEM(wshape, BF16)],
        compiler_params=_params(("arbitrary",)),
        name="out_norm",
    )(*acts, w, h, ln_g.reshape(1, d), ln_b.reshape(1, d))


def _rope_tables(seq, scale_first, scale_second):
    half = HEAD // 2
    inv_freq = ROPE_THETA ** (-jnp.arange(half, dtype=F32) / half)
    ang = jnp.arange(seq, dtype=F32)[:, None] * inv_freq[None, :]
    cos = jnp.cos(ang)
    sin = jnp.sin(ang)
    cos_full = jnp.concatenate([cos, cos], axis=-1)
    sin_full = jnp.concatenate([-sin, sin], axis=-1)
    scales = jnp.array([scale_first, scale_second], F32)[:, None, None]
    return cos_full[None] * scales, sin_full[None] * scales


def kernel(x, ev_w_in, ev_conv_w, ev_conv_b, ev_gate_a_w, ev_gate_a_b, ev_gate_x_w, ev_gate_x_b,
           ev_lru_lambda, ev_ret_gn_g, ev_ret_gn_b, ev_w_out, ev_ln_g, ev_ln_b,
           od_w_in, od_lambda_q1, od_lambda_k1, od_lambda_q2, od_lambda_k2, od_subln_g, od_w_out,
           od_ln_g, od_ln_b):
    batch, seq, d = x.shape
    m = batch * seq
    ret_width = RET_HEADS * HEAD
    ev_cos, ev_sin = _rope_tables(seq, 1.0, HEAD ** -0.5)
    od_cos, od_sin = _rope_tables(seq, HEAD ** -0.5 * LOG2E, 1.0)

    h = x.reshape(m, d).astype(F32)
    hb = h
    for l in range(DEPTH):
        j = l // 2
        if l % 2 == 0:
            proj = _project(hb, ev_w_in, j, ev_cos, ev_sin, 2 * ret_width, seq)
            ret = _retention(proj, ev_ret_gn_g[j], ev_ret_gn_b[j], batch, seq)
            lru = _rg_lru(proj, ev_conv_w[j], ev_conv_b[j], ev_gate_a_w[j], ev_gate_a_b[j],
                          ev_gate_x_w[j], ev_gate_x_b[j], ev_lru_lambda[j], batch, seq)
            h, hb = _out_proj_norm([ret, lru], ev_w_out, j, h, ev_ln_g[j], ev_ln_b[j])
        else:
            lambda_init = 0.8 - 0.6 * math.exp(-0.3 * l)
            proj = _project(hb, od_w_in, j, od_cos, od_sin, 2 * 2 * DIFF_HEADS * HEAD, seq)
            mix = _diff_attention(proj, od_lambda_q1[j], od_lambda_k1[j], od_lambda_q2[j],
                                  od_lambda_k2[j], od_subln_g[j], lambda_init, batch, seq)
            h, hb = _out_proj_norm([mix], od_w_out, j, h, od_ln_g[j], od_ln_b[j])
    return h.reshape(batch, seq, d).astype(x.dtype)
```

```python
import functools
import math

import jax
import jax.numpy as jnp
from jax import lax
from jax.experimental import pallas as pl
from jax.experimental.pallas import tpu as pltpu

F32 = jnp.float32
BF16 = jnp.bfloat16

D_MODEL = 1024
DEPTH = 4
ROPE_THETA = 10000.0
CHUNK = 128
RET_SUB = 4
NORM_EPS = 1e-5
HEAD = 128

RET_HEADS = 8
LRU_WIDTH = 1024
LRU_BLOCKS = 8
CONV_WIDTH = 4
LRU_C = 8.0
DIFF_HEADS = 8
DIFF_DV = 2 * HEAD

DEEPNORM_ALPHA = (2.0 * DEPTH) ** 0.25
LOG2E = math.log2(math.e)

VMEM_LIMIT = 56000 * 1024

PROJ_TM = 1024
PROJ_TN = 2048
PROJ_CHUNK = 512
OUT_TM = 1024
OUT_CHUNK = 256
LRU_T = 128
LRU_NB = 8
LRU_SCAN_UNROLL = 8
ATT_TQ = 512
ATT_TK = 512
ATT_ONES_ROWS = 16


def _params(sem):
    return pltpu.CompilerParams(dimension_semantics=sem, vmem_limit_bytes=VMEM_LIMIT)


def _sigmoid(x):
    return 0.5 * jnp.tanh(0.5 * x) + 0.5


def _sqrt_nonneg(y):
    return jnp.exp2(0.5 * jnp.log2(y))


def _proj_kernel(x_ref, w_ref, cos_ref, sin_ref, o_ref, wb_ref, *, n_rope_blocks, half_chunks, tn):
    j = pl.program_id(0)
    chunks = tn // PROJ_CHUNK

    @pl.when(pl.program_id(1) == 0)
    def _():
        wb_ref[...] = w_ref[...].astype(BF16)

    xb = x_ref[...].astype(BF16)

    @pl.when(j < n_rope_blocks)
    def _():
        for c in range(chunks):
            c0 = c * PROJ_CHUNK
            acc = jnp.dot(xb, wb_ref[:, c0:c0 + PROJ_CHUNK], preferred_element_type=F32)
            tab = (j * chunks + c) // half_chunks
            cos = cos_ref[tab]
            sin = sin_ref[tab]
            for h in range(PROJ_CHUNK // HEAD):
                a = acc[:, h * HEAD:(h + 1) * HEAD]
                r = a * cos + pltpu.roll(a, HEAD // 2, 1) * sin
                o_ref[:, c0 + h * HEAD:c0 + (h + 1) * HEAD] = r.astype(o_ref.dtype)

    @pl.when(j >= n_rope_blocks)
    def _():
        for c in range(chunks):
            c0 = c * PROJ_CHUNK
            acc = jnp.dot(xb, wb_ref[:, c0:c0 + PROJ_CHUNK], preferred_element_type=F32)
            o_ref[:, c0:c0 + PROJ_CHUNK] = acc.astype(o_ref.dtype)


def _project(xb, w, layer, cos_tab, sin_tab, rope_cols, seq):
    m, k = xb.shape
    n = w.shape[2]
    tm, tn = min(PROJ_TM, seq), PROJ_TN
    assert rope_cols % tn == 0 and (rope_cols // 2) % PROJ_CHUNK == 0 and seq % tm == 0
    seq_blocks = seq // tm

    def tab_map(j, i):
        return (0, i % seq_blocks, 0)

    return pl.pallas_call(
        functools.partial(_proj_kernel, n_rope_blocks=rope_cols // tn,
                          half_chunks=rope_cols // 2 // PROJ_CHUNK, tn=tn),
        grid=(n // tn, m // tm),
        in_specs=[
            pl.BlockSpec((tm, k), lambda j, i: (i, 0)),
            pl.BlockSpec((None, k, tn), lambda j, i: (layer, 0, j)),
            pl.BlockSpec((2, tm, HEAD), tab_map),
            pl.BlockSpec((2, tm, HEAD), tab_map),
        ],
        out_specs=pl.BlockSpec((tm, tn), lambda j, i: (i, j)),
        out_shape=jax.ShapeDtypeStruct((m, n), BF16),
        scratch_shapes=[pltpu.VMEM((k, tn), BF16)],
        compiler_params=_params(("arbitrary", "arbitrary")),
        name="proj",
    )(xb, w, cos_tab, sin_tab)


def _retention_kernel(q_ref, k_ref, v_ref, g_ref, decay_ref, qdec_ref, kend_ref, cd_ref,
                      gng_ref, gnb_ref, o_ref, state_ref, mix_ref):
    n = pl.program_id(1)
    c = CHUNK

    @pl.when(n == 0)
    def _():
        state_ref[...] = jnp.zeros_like(state_ref)

    blk_r = lax.broadcasted_iota(jnp.int32, (2 * HEAD, 2 * HEAD), 0) // HEAD
    blk_c = lax.broadcasted_iota(jnp.int32, (2 * HEAD, 2 * HEAD), 1) // HEAD
    on_block = blk_r == blk_c

    states = [state_ref[p] for p in range(RET_HEADS // 2)]
    zeros = jnp.zeros((c, HEAD), BF16)

    def mix(rows):
        for p in range(RET_HEADS // 2):
            h0 = slice(2 * p * HEAD, (2 * p + 1) * HEAD)
            h1 = slice((2 * p + 1) * HEAD, (2 * p + 2) * HEAD)
            pair = slice(2 * p * HEAD, (2 * p + 2) * HEAD)
            q_pair = q_ref[rows, pair]
            q_rows = jnp.concatenate([q_ref[rows, h0], q_ref[rows, h1]], axis=0)
            k_rows = jnp.concatenate([k_ref[rows, h0], k_ref[rows, h1]], axis=0)
            s_full = lax.dot_general(q_rows, k_rows, (((1,), (1,)), ((), ())),
                                     preferred_element_type=F32)
            s_pair = jnp.concatenate([s_full[0:c, 0:c] * decay_ref[2 * p],
                                      s_full[c:2 * c, c:2 * c] * decay_ref[2 * p + 1]], axis=1)
            v_pair = v_ref[rows, pair]
            v_bd = jnp.concatenate(
                [jnp.concatenate([v_pair[:, 0:HEAD], zeros], axis=1),
                 jnp.concatenate([zeros, v_pair[:, HEAD:2 * HEAD]], axis=1)], axis=0)
            intra = jnp.dot(s_pair.astype(BF16), v_bd, preferred_element_type=F32)
            state = states[p]
            cross = jnp.dot(q_pair, state.astype(BF16), preferred_element_type=F32)
            v_end = (v_pair.astype(F32) * kend_ref[:, pair]).astype(BF16)
            kv = lax.dot_general(k_ref[rows, pair], v_end, (((0,), (0,)), ((), ())),
                                 preferred_element_type=F32)
            states[p] = state * cd_ref[p] + jnp.where(on_block, kv, 0.0)
            mix_ref[rows, pair] = intra + cross * qdec_ref[:, pair]

    def norm_gate(rows):
        for h in range(RET_HEADS):
            hs = slice(h * HEAD, (h + 1) * HEAD)
            oh = mix_ref[rows, hs]
            mu = jnp.mean(oh, axis=-1, keepdims=True)
            d = oh - mu
            var = jnp.mean(d * d, axis=-1, keepdims=True)
            oh = d * lax.rsqrt(var + NORM_EPS) * gng_ref[:, hs] + gnb_ref[:, hs]
            g = g_ref[rows, hs].astype(F32)
            oh = oh * (g * _sigmoid(g))
            o_ref[rows, hs] = oh.astype(o_ref.dtype)

    for sub in range(RET_SUB):
        mix(slice(sub * c, (sub + 1) * c))
        if sub > 0:
            norm_gate(slice((sub - 1) * c, sub * c))
    norm_gate(slice((RET_SUB - 1) * c, RET_SUB * c))
    for p in range(RET_HEADS // 2):
        state_ref[p] = states[p]


def _retention(proj, gn_g, gn_b, batch, seq):
    m = proj.shape[0]
    rows_blk = RET_SUB * CHUNK
    nchunk = seq // rows_blk
    width = RET_HEADS * HEAD
    log_g = jnp.log(1.0 - 2.0 ** (-5.0 - jnp.arange(RET_HEADS, dtype=F32)))
    idx = jnp.arange(CHUNK, dtype=F32)
    rel = idx[:, None] - idx[None, :]
    decay = jnp.where(rel[None] >= 0,
                      jnp.exp(log_g[:, None, None] * jnp.maximum(rel, 0.0)[None]), 0.0)
    kend = jnp.exp(log_g[:, None] * (CHUNK - 1 - idx)[None, :])
    qdec = jnp.exp(log_g[:, None] * (idx + 1.0)[None, :])
    kend = jnp.repeat(kend.T, HEAD, axis=1)
    qdec = jnp.repeat(qdec.T, HEAD, axis=1)
    cd = jnp.exp(log_g * CHUNK)
    npair = RET_HEADS // 2
    blk = jnp.arange(2 * HEAD) // HEAD
    cd_bd = jnp.where(blk[None, :, None] == blk[None, None, :],
                      cd.reshape(npair, 2)[:, blk][:, :, None], 0.0).astype(F32)

    def row(b, n):
        return b * nchunk + n

    def col_spec(c):
        return pl.BlockSpec((rows_blk, width), lambda b, n, c=c: (row(b, n), c))

    def const(shape):
        return pl.BlockSpec(shape, lambda b, n: (0,) * len(shape))

    return pl.pallas_call(
        _retention_kernel,
        grid=(batch, nchunk),
        in_specs=[
            col_spec(0), col_spec(1), col_spec(2), col_spec(3),
            const((RET_HEADS, CHUNK, CHUNK)), const((CHUNK, width)), const((CHUNK, width)),
            const((npair, 2 * HEAD, 2 * HEAD)), const((1, width)), const((1, width)),
        ],
        out_specs=pl.BlockSpec((rows_blk, width), lambda b, n: (row(b, n), 0)),
        out_shape=jax.ShapeDtypeStruct((m, width), BF16),
        scratch_shapes=[pltpu.VMEM((npair, 2 * HEAD, 2 * HEAD), F32),
                        pltpu.VMEM((rows_blk, width), F32)],
        compiler_params=_params(("arbitrary", "arbitrary")),
        name="retention",
    )(proj, proj, proj, proj, decay, qdec, kend, cd_bd,
      gn_g.reshape(1, width), gn_b.reshape(1, width))


def _softplus(z):
    return jnp.maximum(z, 0.0) + jnp.log1p(jnp.exp(-jnp.abs(z)))


def _lru_kernel(x_ref, g_ref, cw_ref, cb_ref, wa_ref, ba_ref, wx_ref, bx_ref, lam_ref,
                o_ref, xs_ref, hist_ref, a_ref, u_ref, h_ref, carry_ref, *, t_blk):
    t = pl.program_id(1)
    nb = LRU_NB
    bd = LRU_WIDTH // LRU_BLOCKS
    rows = t_blk * nb
    hist = (CONV_WIDTH - 1) * nb

    @pl.when(t == 0)
    def _():
        hist_ref[...] = jnp.zeros_like(hist_ref)
        carry_ref[...] = jnp.zeros_like(carry_ref)

    for b in range(nb):
        xb = x_ref[b].astype(F32)
        for blk in range(LRU_BLOCKS):
            xs_ref[blk, pl.ds(hist + b, t_blk, stride=nb), :] = xb[:, blk * bd:(blk + 1) * bd]

    neg_c_sp = -LRU_C * _softplus(-lam_ref[...])
    for blk in range(LRU_BLOCKS):
        cs = slice(blk * bd, (blk + 1) * bd)
        xs_ref[blk, 0:hist, :] = hist_ref[blk]
        xc = cb_ref[:, cs] + jnp.zeros((rows, bd), F32)
        for w in range(CONV_WIDTH):
            xc = xc + cw_ref[w:w + 1, cs] * xs_ref[blk, w * nb:w * nb + rows, :]
        hist_ref[blk] = xs_ref[blk, rows:rows + hist, :]
        xgb = xc.astype(BF16)
        r = _sigmoid(jnp.dot(xgb, wa_ref[blk], preferred_element_type=F32) + ba_ref[:, cs])
        i = _sigmoid(jnp.dot(xgb, wx_ref[blk], preferred_element_type=F32) + bx_ref[:, cs])
        log_a = r * neg_c_sp[:, cs]
        a = jnp.exp(log_a)
        a_ref[blk] = a
        u_ref[blk] = xc * i * _sqrt_nonneg(-jnp.tanh(log_a) * (a * a + 1.0))

    def body(tt, hs):
        r0 = pl.multiple_of(tt * nb, nb)
        out = []
        for blk in range(LRU_BLOCKS):
            h = a_ref[blk, pl.ds(r0, nb), :] * hs[blk] + u_ref[blk, pl.ds(r0, nb), :]
            h_ref[blk, pl.ds(r0, nb), :] = h
            out.append(h)
        return tuple(out)

    hs = lax.fori_loop(0, t_blk, body, tuple(carry_ref[blk] for blk in range(LRU_BLOCKS)),
                       unroll=LRU_SCAN_UNROLL)
    for blk in range(LRU_BLOCKS):
        carry_ref[blk] = hs[blk]

    for b in range(nb):
        h = jnp.concatenate([h_ref[blk, pl.ds(b, t_blk, stride=nb), :]
                             for blk in range(LRU_BLOCKS)], axis=1)
        g = g_ref[b].astype(F32)
        o_ref[b] = (h * (g * _sigmoid(g))).astype(o_ref.dtype)


def _rg_lru(proj, conv_w, conv_b, wa, ba, wx, bx, lam, batch, seq):
    m, n = proj.shape
    t_blk = min(LRU_T, seq)
    nb = LRU_NB
    assert batch % nb == 0 and seq % t_blk == 0
    w = LRU_WIDTH
    bd = w // LRU_BLOCKS
    proj3 = proj.reshape(batch, seq, n)

    def col_spec(c):
        return pl.BlockSpec((nb, t_blk, w), lambda b, t, c=c: (b, t, c))

    vec = pl.BlockSpec((1, w), lambda b, t: (0, 0))
    gate_w = pl.BlockSpec((LRU_BLOCKS, bd, bd), lambda b, t: (0, 0, 0))
    out = pl.pallas_call(
        functools.partial(_lru_kernel, t_blk=t_blk),
        grid=(batch // nb, seq // t_blk),
        in_specs=[
            col_spec(4), col_spec(5),
            pl.BlockSpec((CONV_WIDTH, w), lambda b, t: (0, 0)), vec,
            gate_w, vec, gate_w, vec, vec,
        ],
        out_specs=pl.BlockSpec((nb, t_blk, w), lambda b, t: (b, t, 0)),
        out_shape=jax.ShapeDtypeStruct((batch, seq, w), BF16),
        scratch_shapes=[
            pltpu.VMEM((LRU_BLOCKS, (t_blk + CONV_WIDTH - 1) * nb, bd), F32),
            pltpu.VMEM((LRU_BLOCKS, (CONV_WIDTH - 1) * nb, bd), F32),
            pltpu.VMEM((LRU_BLOCKS, t_blk * nb, bd), F32),
            pltpu.VMEM((LRU_BLOCKS, t_blk * nb, bd), F32),
            pltpu.VMEM((LRU_BLOCKS, t_blk * nb, bd), F32),
            pltpu.VMEM((LRU_BLOCKS, nb, bd), F32),
        ],
        compiler_params=_params(("arbitrary", "arbitrary")),
        name="rg_lru",
    )(proj3, proj3, conv_w, conv_b.reshape(1, w), wa.astype(BF16), ba.reshape(1, w),
      wx.astype(BF16), bx.reshape(1, w), lam.reshape(1, w))
    return out.reshape(m, w)


def _attn_kernel(q_ref, k_ref, v_ref, g_ref, bias_ref, lq1_ref, lk1_ref, lq2_ref, lk2_ref,
                 sg_ref, o_ref, vt_ref, acc_ref, s_ref, *, seq, lambda_init):
    tq = ATT_TQ
    tk = ATT_TK
    for c in range(seq // tk):
        vt_ref[c, 0:DIFF_DV, :] = v_ref[c * tk:(c + 1) * tk, :].T
        vt_ref[c, DIFF_DV:, :] = jnp.ones((ATT_ONES_ROWS, tk), BF16)

    lam = (jnp.exp(jnp.sum(lq1_ref[...] * lk1_ref[...], axis=-1, keepdims=True))
           - jnp.exp(jnp.sum(lq2_ref[...] * lk2_ref[...], axis=-1, keepdims=True))
           + lambda_init)

    half = tk // 2
    steps = []
    for qi in range(seq // tq):
        for j in range(qi):
            steps.append((qi, j * tk, tk, 0, False, j == 0, False))
        steps.append((qi, qi * tk, half, 0, True, qi == 0, False))
        steps.append((qi, qi * tk + half, half, half, True, False, True))

    def issue_scores(st, slot):
        qi, k0, nk, qlo, _, _, _ = st
        qblk = q_ref[qi * tq + qlo:(qi + 1) * tq, :]
        kblk = k_ref[k0:k0 + nk, :]
        for c in range(2):
            s_ref[slot, c, 0:nk, 0:tq - qlo] = lax.dot_general(
                kblk[:, c * HEAD:(c + 1) * HEAD], qblk[:, c * HEAD:(c + 1) * HEAD],
                (((1,), (1,)), ((), ())), preferred_element_type=F32)

    def consume(st, slot, m):
        qi, k0, nk, qlo, masked, first, _ = st
        nq = tq - qlo
        par = qi % 2
        for c in range(2):
            s = s_ref[slot, c, 0:nk, 0:nq]
            if masked:
                s = s + bias_ref[:, 0:nq]
            mx = jnp.max(s, axis=0, keepdims=True)
            m_old = None if first else m[c][:, qlo:]
            m_new = mx if first else jnp.maximum(m_old, mx)
            p = jnp.exp2((s - m_new).astype(BF16))
            vt = vt_ref[k0 // tk, :, k0 % tk:k0 % tk + nk]
            pv = jnp.dot(vt, p, preferred_element_type=F32)
            if first:
                acc_ref[par, c] = pv
            else:
                acc_ref[par, c, :, qlo:] = (jnp.exp2(m_old - m_new) * acc_ref[par, c, :, qlo:]
                                            + pv)
            m[c] = m_new if qlo == 0 else jnp.concatenate([m[c][:, :qlo], m_new], axis=1)

    sg_scaled = sg_ref[...] * (1.0 - lambda_init)

    def finalize(qi, part):
        dv = DIFF_DV
        par = qi % 2
        ql = slice(part * half, (part + 1) * half)
        w0 = 1.0 / acc_ref[par, 0, dv:dv + 1, ql]
        w1 = lam * (1.0 / acc_ref[par, 1, dv:dv + 1, ql])
        o_t = acc_ref[par, 0, 0:dv, ql] * w0 - acc_ref[par, 1, 0:dv, ql] * w1
        ms = jnp.mean(o_t * o_t, axis=0, keepdims=True)
        o = (o_t * lax.rsqrt(ms + NORM_EPS)).T
        rows = slice(qi * tq + part * half, qi * tq + (part + 1) * half)
        g = g_ref[rows, :].astype(F32)
        o_ref[rows, :] = ((o * sg_scaled) * (g * _sigmoid(g))).astype(o_ref.dtype)

    issue_scores(steps[0], 0)
    m = [None, None]
    pending = []
    for i, st in enumerate(steps):
        if i + 1 < len(steps):
            issue_scores(steps[i + 1], (i + 1) % 2)
        consume(st, i % 2, m)
        if pending:
            finalize(*pending.pop(0))
        if st[6]:
            pending += [(st[0], 0), (st[0], 1)]
    for job in pending:
        finalize(*job)


def _diff_attention(proj, lq1, lk1, lq2, lk2, subln_g, lambda_init, batch, seq):
    m = proj.shape[0]
    tq, tk = ATT_TQ, ATT_TK
    assert tq == tk and seq % tk == 0
    wblk = DIFF_DV
    per_region = DIFF_HEADS
    kk = jnp.arange(tk // 2, dtype=jnp.int32)[:, None]
    qq = jnp.arange(tq, dtype=jnp.int32)[None, :]
    bias = jnp.where(kk <= qq, 0.0, -jnp.inf).astype(F32)

    def blk(region):
        return pl.BlockSpec((seq, wblk), lambda b, h, r=region: (b, r * per_region + h))

    vec = pl.BlockSpec((1, HEAD), lambda b, h: (0, 0))
    return pl.pallas_call(
        functools.partial(_attn_kernel, seq=seq, lambda_init=lambda_init),
        grid=(batch, DIFF_HEADS),
        in_specs=[
            blk(0), blk(1), blk(2), blk(3),
            pl.BlockSpec((tk // 2, tq), lambda b, h: (0, 0)),
            vec, vec, vec, vec,
            pl.BlockSpec((1, wblk), lambda b, h: (0, 0)),
        ],
        out_specs=pl.BlockSpec((seq, wblk), lambda b, h: (b, h)),
        out_shape=jax.ShapeDtypeStruct((m, DIFF_HEADS * wblk), BF16),
        scratch_shapes=[
            pltpu.VMEM((seq // tk, wblk + ATT_ONES_ROWS, tk), BF16),
            pltpu.VMEM((2, 2, wblk + ATT_ONES_ROWS, tq), F32),
            pltpu.VMEM((2, 2, tk, tq), F32),
        ],
        compiler_params=_params(("arbitrary", "arbitrary")),
        name="diff_attn",
    )(proj, proj, proj, proj, bias, lq1.reshape(1, HEAD), lk1.reshape(1, HEAD),
      lq2.reshape(1, HEAD), lk2.reshape(1, HEAD), subln_g.reshape(1, wblk))


def _out_kernel(*refs, n_in):
    acts = refs[:n_in]
    w_ref, h_ref, g_ref, b_ref, o_ref, ob_ref, wb_ref = refs[n_in:]

    @pl.when(pl.program_id(0) == 0)
    def _():
        wb_ref[...] = w_ref[...].astype(BF16)

    for r in range(h_ref.shape[0] // OUT_CHUNK):
        rows = slice(r * OUT_CHUNK, (r + 1) * OUT_CHUNK)
        y = None
        k0 = 0
        for a in acts:
            kw = a.shape[1]
            part = jnp.dot(a[rows, :], wb_ref[k0:k0 + kw, :], preferred_element_type=F32)
            y = part if y is None else y + part
            k0 += kw
        z = DEEPNORM_ALPHA * h_ref[rows, :] + y
        mu = jnp.mean(z, axis=-1, keepdims=True)
        d = z - mu
        var = jnp.mean(d * d, axis=-1, keepdims=True)
        out = d * lax.rsqrt(var + NORM_EPS) * g_ref[...] + b_ref[...]
        o_ref[rows, :] = out
        ob_ref[rows, :] = out.astype(BF16)


def _out_proj_norm(acts, w, layer, h, ln_g, ln_b):
    m, d = h.shape
    tm = OUT_TM
    n_in = len(acts)
    wshape = w.shape[1:]
    assert sum(a.shape[1] for a in acts) == wshape[0]
    in_specs = [pl.BlockSpec((tm, a.shape[1]), lambda i: (i, 0)) for a in acts]
    in_specs += [pl.BlockSpec((None,) + wshape, lambda i: (layer, 0, 0)),
                 pl.BlockSpec((tm, d), lambda i: (i, 0)),
                 pl.BlockSpec((1, d), lambda i: (0, 0)),
                 pl.BlockSpec((1, d), lambda i: (0, 0))]
    return pl.pallas_call(
        functools.partial(_out_kernel, n_in=n_in),
        grid=(m // tm,),
        in_specs=in_specs,
        out_specs=[pl.BlockSpec((tm, d), lambda i: (i, 0)),
                   pl.BlockSpec((tm, d), lambda i: (i, 0))],
        out_shape=[jax.ShapeDtypeStruct((m, d), F32), jax.ShapeDtypeStruct((m, d), BF16)],
        scratch_shapes=[pltpu.VMEM(wshape, BF16)],
        compiler_params=_params(("arbitrary",)),
        name="out_norm",
    )(*acts, w, h, ln_g.reshape(1, d), ln_b.reshape(1, d))


def _rope_tables(seq, scale_first, scale_second):
    half = HEAD // 2
    inv_freq = ROPE_THETA ** (-jnp.arange(half, dtype=F32) / half)
    ang = jnp.arange(seq, dtype=F32)[:, None] * inv_freq[None, :]
    cos = jnp.cos(ang)
    sin = jnp.sin(ang)
    cos_full = jnp.concatenate([cos, cos], axis=-1)
    sin_full = jnp.concatenate([-sin, sin], axis=-1)
    scales = jnp.array([scale_first, scale_second], F32)[:, None, None]
    return cos_full[None] * scales, sin_full[None] * scales


def kernel(x, ev_w_in, ev_conv_w, ev_conv_b, ev_gate_a_w, ev_gate_a_b, ev_gate_x_w, ev_gate_x_b,
           ev_lru_lambda, ev_ret_gn_g, ev_ret_gn_b, ev_w_out, ev_ln_g, ev_ln_b,
           od_w_in, od_lambda_q1, od_lambda_k1, od_lambda_q2, od_lambda_k2, od_subln_g, od_w_out,
           od_ln_g, od_ln_b):
    batch, seq, d = x.shape
    m = batch * seq
    ret_width = RET_HEADS * HEAD
    ev_cos, ev_sin = _rope_tables(seq, 1.0, HEAD ** -0.5)
    od_cos, od_sin = _rope_tables(seq, HEAD ** -0.5 * LOG2E, 1.0)

    h = x.reshape(m, d).astype(F32)
    hb = h
    for l in range(DEPTH):
        j = l // 2
        if l % 2 == 0:
            proj = _project(hb, ev_w_in, j, ev_cos, ev_sin, 2 * ret_width, seq)
            ret = _retention(proj, ev_ret_gn_g[j], ev_ret_gn_b[j], batch, seq)
            lru = _rg_lru(proj, ev_conv_w[j], ev_conv_b[j], ev_gate_a_w[j], ev_gate_a_b[j],
                          ev_gate_x_w[j], ev_gate_x_b[j], ev_lru_lambda[j], batch, seq)
            h, hb = _out_proj_norm([ret, lru], ev_w_out, j, h, ev_ln_g[j], ev_ln_b[j])
        else:
            lambda_init = 0.8 - 0.6 * math.exp(-0.3 * l)
            proj = _project(hb, od_w_in, j, od_cos, od_sin, 2 * 2 * DIFF_HEADS * HEAD, seq)
            mix = _diff_attention(proj, od_lambda_q1[j], od_lambda_k1[j], od_lambda_q2[j],
                                  od_lambda_k2[j], od_subln_g[j], lambda_init, batch, seq)
            h, hb = _out_proj_norm([mix], od_w_out, j, h, od_ln_g[j], od_ln_b[j])
    return h.reshape(batch, seq, d).astype(x.dtype)
```

```python
import functools
import math

import jax
import jax.numpy as jnp
from jax import lax
from jax.experimental import pallas as pl
from jax.experimental.pallas import tpu as pltpu

F32 = jnp.float32
BF16 = jnp.bfloat16

D_MODEL = 1024
DEPTH = 4
ROPE_THETA = 10000.0
CHUNK = 128
RET_SUB = 8
NORM_EPS = 1e-5
HEAD = 128

RET_HEADS = 8
LRU_WIDTH = 1024
LRU_BLOCKS = 8
CONV_WIDTH = 4
LRU_C = 8.0
DIFF_HEADS = 8
DIFF_DV = 2 * HEAD

DEEPNORM_ALPHA = (2.0 * DEPTH) ** 0.25
LOG2E = math.log2(math.e)

VMEM_LIMIT = 56000 * 1024

PROJ_TM = 1024
PROJ_TN = 2048
PROJ_CHUNK = 512
OUT_TM = 1024
OUT_CHUNK = 256
LRU_T = 128
LRU_NB = 8
LRU_SCAN_UNROLL = 8
ATT_TQ = 512
ATT_TK = 512
ATT_ONES_ROWS = 16


def _params(sem):
    return pltpu.CompilerParams(dimension_semantics=sem, vmem_limit_bytes=VMEM_LIMIT)


def _silu(x):
    hx = 0.5 * x
    return hx * jnp.tanh(hx) + hx


def _sqrt_nonneg(y):
    return jnp.exp2(0.5 * jnp.log2(y))


def _proj_kernel(x_ref, w_ref, cos_ref, sin_ref, o_ref, wb_ref, *, n_rope_blocks, half_chunks, tn):
    j = pl.program_id(0)
    chunks = tn // PROJ_CHUNK

    @pl.when(pl.program_id(1) == 0)
    def _():
        wb_ref[...] = w_ref[...].astype(BF16)

    xb = x_ref[...].astype(BF16)

    @pl.when(j < n_rope_blocks)
    def _():
        for c in range(chunks):
            c0 = c * PROJ_CHUNK
            acc = jnp.dot(xb, wb_ref[:, c0:c0 + PROJ_CHUNK], preferred_element_type=F32)
            tab = (j * chunks + c) // half_chunks
            cos = cos_ref[tab]
            sin = sin_ref[tab]
            for h in range(PROJ_CHUNK // HEAD):
                a = acc[:, h * HEAD:(h + 1) * HEAD]
                r = a * cos + pltpu.roll(a, HEAD // 2, 1) * sin
                o_ref[:, c0 + h * HEAD:c0 + (h + 1) * HEAD] = r.astype(o_ref.dtype)

    @pl.when(j >= n_rope_blocks)
    def _():
        for c in range(chunks):
            c0 = c * PROJ_CHUNK
            acc = jnp.dot(xb, wb_ref[:, c0:c0 + PROJ_CHUNK], preferred_element_type=F32)
            o_ref[:, c0:c0 + PROJ_CHUNK] = acc.astype(o_ref.dtype)


def _project(xb, w, layer, cos_tab, sin_tab, rope_cols, seq):
    m, k = xb.shape
    n = w.shape[2]
    tm, tn = min(PROJ_TM, seq), PROJ_TN
    assert rope_cols % tn == 0 and (rope_cols // 2) % PROJ_CHUNK == 0 and seq % tm == 0
    seq_blocks = seq // tm

    def tab_map(j, i):
        return (0, i % seq_blocks, 0)

    return pl.pallas_call(
        functools.partial(_proj_kernel, n_rope_blocks=rope_cols // tn,
                          half_chunks=rope_cols // 2 // PROJ_CHUNK, tn=tn),
        grid=(n // tn, m // tm),
        in_specs=[
            pl.BlockSpec((tm, k), lambda j, i: (i, 0)),
            pl.BlockSpec((None, k, tn), lambda j, i: (layer, 0, j)),
            pl.BlockSpec((2, tm, HEAD), tab_map),
            pl.BlockSpec((2, tm, HEAD), tab_map),
        ],
        out_specs=pl.BlockSpec((tm, tn), lambda j, i: (i, j)),
        out_shape=jax.ShapeDtypeStruct((m, n), BF16),
        scratch_shapes=[pltpu.VMEM((k, tn), BF16)],
        compiler_params=_params(("arbitrary", "arbitrary")),
        name="proj",
    )(xb, w, cos_tab, sin_tab)


def _retention_kernel(q_ref, k_ref, v_ref, g_ref, decay_ref, qdec_ref, kend_ref, cd_ref,
                      gng_ref, gnb_ref, o_ref, state_ref, mix_ref):
    n = pl.program_id(1)
    c = CHUNK

    @pl.when(n == 0)
    def _():
        state_ref[...] = jnp.zeros_like(state_ref)

    blk_r = lax.broadcasted_iota(jnp.int32, (2 * HEAD, 2 * HEAD), 0) // HEAD
    blk_c = lax.broadcasted_iota(jnp.int32, (2 * HEAD, 2 * HEAD), 1) // HEAD
    on_block = blk_r == blk_c

    states = [state_ref[p] for p in range(RET_HEADS // 2)]
    zeros = jnp.zeros((c, HEAD), BF16)

    def mix(rows):
        for p in range(RET_HEADS // 2):
            h0 = slice(2 * p * HEAD, (2 * p + 1) * HEAD)
            h1 = slice((2 * p + 1) * HEAD, (2 * p + 2) * HEAD)
            pair = slice(2 * p * HEAD, (2 * p + 2) * HEAD)
            q_pair = q_ref[rows, pair]
            q_rows = jnp.concatenate([q_ref[rows, h0], q_ref[rows, h1]], axis=0)
            k_rows = jnp.concatenate([k_ref[rows, h0], k_ref[rows, h1]], axis=0)
            s_full = lax.dot_general(q_rows, k_rows, (((1,), (1,)), ((), ())),
                                     preferred_element_type=F32)
            s_pair = jnp.concatenate([s_full[0:c, 0:c] * decay_ref[2 * p],
                                      s_full[c:2 * c, c:2 * c] * decay_ref[2 * p + 1]], axis=1)
            v_pair = v_ref[rows, pair]
            v_bd = jnp.concatenate(
                [jnp.concatenate([v_pair[:, 0:HEAD], zeros], axis=1),
                 jnp.concatenate([zeros, v_pair[:, HEAD:2 * HEAD]], axis=1)], axis=0)
            intra = jnp.dot(s_pair.astype(BF16), v_bd, preferred_element_type=F32)
            state = states[p]
            cross = jnp.dot(q_pair, state.astype(BF16), preferred_element_type=F32)
            v_end = (v_pair.astype(F32) * kend_ref[:, pair]).astype(BF16)
            kv = lax.dot_general(k_ref[rows, pair], v_end, (((0,), (0,)), ((), ())),
                                 preferred_element_type=F32)
            states[p] = state * cd_ref[p] + jnp.where(on_block, kv, 0.0)
            mix_ref[rows, pair] = intra + cross * qdec_ref[:, pair]

    def norm_gate(rows):
        for h in range(RET_HEADS):
            hs = slice(h * HEAD, (h + 1) * HEAD)
            oh = mix_ref[rows, hs]
            mu = jnp.mean(oh, axis=-1, keepdims=True)
            d = oh - mu
            var = jnp.mean(d * d, axis=-1, keepdims=True)
            oh = d * lax.rsqrt(var + NORM_EPS) * gng_ref[:, hs] + gnb_ref[:, hs]
            g = g_ref[rows, hs].astype(F32)
            oh = oh * _silu(g)
            o_ref[rows, hs] = oh.astype(o_ref.dtype)

    for sub in range(RET_SUB):
        mix(slice(sub * c, (sub + 1) * c))
        if sub > 0:
            norm_gate(slice((sub - 1) * c, sub * c))
    norm_gate(slice((RET_SUB - 1) * c, RET_SUB * c))
    for p in range(RET_HEADS // 2):
        state_ref[p] = states[p]


def _retention(proj, gn_g, gn_b, batch, seq):
    m = proj.shape[0]
    rows_blk = RET_SUB * CHUNK
    nchunk = seq // rows_blk
    width = RET_HEADS * HEAD
    log_g = jnp.log(1.0 - 2.0 ** (-5.0 - jnp.arange(RET_HEADS, dtype=F32)))
    idx = jnp.arange(CHUNK, dtype=F32)
    rel = idx[:, None] - idx[None, :]
    decay = jnp.where(rel[None] >= 0,
                      jnp.exp(log_g[:, None, None] * jnp.maximum(rel, 0.0)[None]), 0.0)
    kend = jnp.exp(log_g[:, None] * (CHUNK - 1 - idx)[None, :])
    qdec = jnp.exp(log_g[:, None] * (idx + 1.0)[None, :])
    kend = jnp.repeat(kend.T, HEAD, axis=1)
    qdec = jnp.repeat(qdec.T, HEAD, axis=1)
    cd = jnp.exp(log_g * CHUNK)
    npair = RET_HEADS // 2
    blk = jnp.arange(2 * HEAD) // HEAD
    cd_bd = jnp.where(blk[None, :, None] == blk[None, None, :],
                      cd.reshape(npair, 2)[:, blk][:, :, None], 0.0).astype(F32)

    def row(b, n):
        return b * nchunk + n

    def col_spec(c):
        return pl.BlockSpec((rows_blk, width), lambda b, n, c=c: (row(b, n), c))

    def const(shape):
        return pl.BlockSpec(shape, lambda b, n: (0,) * len(shape))

    return pl.pallas_call(
        _retention_kernel,
        grid=(batch, nchunk),
        in_specs=[
            col_spec(0), col_spec(1), col_spec(2), col_spec(3),
            const((RET_HEADS, CHUNK, CHUNK)), const((CHUNK, width)), const((CHUNK, width)),
            const((npair, 2 * HEAD, 2 * HEAD)), const((1, width)), const((1, width)),
        ],
        out_specs=pl.BlockSpec((rows_blk, width), lambda b, n: (row(b, n), 0)),
        out_shape=jax.ShapeDtypeStruct((m, width), BF16),
        scratch_shapes=[pltpu.VMEM((npair, 2 * HEAD, 2 * HEAD), F32),
                        pltpu.VMEM((rows_blk, width), F32)],
        compiler_params=_params(("arbitrary", "arbitrary")),
        name="retention",
    )(proj, proj, proj, proj, decay, qdec, kend, cd_bd,
      gn_g.reshape(1, width), gn_b.reshape(1, width))


def _softplus(z):
    return jnp.maximum(z, 0.0) + jnp.log1p(jnp.exp(-jnp.abs(z)))


def _lru_kernel(x_ref, g_ref, cw_ref, cb_ref, wa_ref, ba_ref, wx_ref, bx_ref, lam_ref,
                o_ref, xs_ref, hist_ref, a_ref, u_ref, h_ref, carry_ref, *, t_blk):
    t = pl.program_id(1)
    nb = LRU_NB
    bd = LRU_WIDTH // LRU_BLOCKS
    rows = t_blk * nb
    hist = (CONV_WIDTH - 1) * nb

    @pl.when(t == 0)
    def _():
        hist_ref[...] = jnp.zeros_like(hist_ref)
        carry_ref[...] = jnp.zeros_like(carry_ref)

    for b in range(nb):
        xb = x_ref[b].astype(F32)
        for blk in range(LRU_BLOCKS):
            xs_ref[blk, pl.ds(hist + b, t_blk, stride=nb), :] = xb[:, blk * bd:(blk + 1) * bd]

    half_sp = (-0.5 * LRU_C) * _softplus(-lam_ref[...])
    for blk in range(LRU_BLOCKS):
        cs = slice(blk * bd, (blk + 1) * bd)
        xs_ref[blk, 0:hist, :] = hist_ref[blk]
        xc = cb_ref[:, cs] + jnp.zeros((rows, bd), F32)
        for w in range(CONV_WIDTH):
            xc = xc + cw_ref[w:w + 1, cs] * xs_ref[blk, w * nb:w * nb + rows, :]
        hist_ref[blk] = xs_ref[blk, rows:rows + hist, :]
        xgb = xc.astype(BF16)
        tr = jnp.tanh(jnp.dot(xgb, wa_ref[blk], preferred_element_type=F32) + ba_ref[:, cs])
        ti = jnp.tanh(jnp.dot(xgb, wx_ref[blk], preferred_element_type=F32) + bx_ref[:, cs])
        i = 0.5 * ti + 0.5
        log_a = tr * half_sp[:, cs] + half_sp[:, cs]
        a = jnp.exp(log_a)
        a_ref[blk] = a
        u_ref[blk] = xc * i * _sqrt_nonneg(-jnp.tanh(log_a) * (a * a + 1.0))

    def body(tt, hs):
        r0 = pl.multiple_of(tt * nb, nb)
        out = []
        for blk in range(LRU_BLOCKS):
            h = a_ref[blk, pl.ds(r0, nb), :] * hs[blk] + u_ref[blk, pl.ds(r0, nb), :]
            h_ref[blk, pl.ds(r0, nb), :] = h
            out.append(h)
        return tuple(out)

    hs = lax.fori_loop(0, t_blk, body, tuple(carry_ref[blk] for blk in range(LRU_BLOCKS)),
                       unroll=LRU_SCAN_UNROLL)
    for blk in range(LRU_BLOCKS):
        carry_ref[blk] = hs[blk]

    for b in range(nb):
        h = jnp.concatenate([h_ref[blk, pl.ds(b, t_blk, stride=nb), :]
                             for blk in range(LRU_BLOCKS)], axis=1)
        g = g_ref[b].astype(F32)
        o_ref[b] = (h * _silu(g)).astype(o_ref.dtype)


def _rg_lru(proj, conv_w, conv_b, wa, ba, wx, bx, lam, batch, seq):
    m, n = proj.shape
    t_blk = min(LRU_T, seq)
    nb = LRU_NB
    assert batch % nb == 0 and seq % t_blk == 0
    w = LRU_WIDTH
    bd = w // LRU_BLOCKS
    proj3 = proj.reshape(batch, seq, n)

    def col_spec(c):
        return pl.BlockSpec((nb, t_blk, w), lambda b, t, c=c: (b, t, c))

    vec = pl.BlockSpec((1, w), lambda b, t: (0, 0))
    gate_w = pl.BlockSpec((LRU_BLOCKS, bd, bd), lambda b, t: (0, 0, 0))
    out = pl.pallas_call(
        functools.partial(_lru_kernel, t_blk=t_blk),
        grid=(batch // nb, seq // t_blk),
        in_specs=[
            col_spec(4), col_spec(5),
            pl.BlockSpec((CONV_WIDTH, w), lambda b, t: (0, 0)), vec,
            gate_w, vec, gate_w, vec, vec,
        ],
        out_specs=pl.BlockSpec((nb, t_blk, w), lambda b, t: (b, t, 0)),
        out_shape=jax.ShapeDtypeStruct((batch, seq, w), BF16),
        scratch_shapes=[
            pltpu.VMEM((LRU_BLOCKS, (t_blk + CONV_WIDTH - 1) * nb, bd), F32),
            pltpu.VMEM((LRU_BLOCKS, (CONV_WIDTH - 1) * nb, bd), F32),
            pltpu.VMEM((LRU_BLOCKS, t_blk * nb, bd), F32),
            pltpu.VMEM((LRU_BLOCKS, t_blk * nb, bd), F32),
            pltpu.VMEM((LRU_BLOCKS, t_blk * nb, bd), F32),
            pltpu.VMEM((LRU_BLOCKS, nb, bd), F32),
        ],
        compiler_params=_params(("arbitrary", "arbitrary")),
        name="rg_lru",
    )(proj3, proj3, conv_w, conv_b.reshape(1, w), (0.5 * wa).astype(BF16), 0.5 * ba.reshape(1, w),
      (0.5 * wx).astype(BF16), 0.5 * bx.reshape(1, w), lam.reshape(1, w))
    return out.reshape(m, w)


def _attn_kernel(q_ref, k_ref, v_ref, g_ref, bias_ref, lq1_ref, lk1_ref, lq2_ref, lk2_ref,
                 sg_ref, o_ref, vt_ref, acc_ref, s_ref, *, seq, lambda_init):
    tq = ATT_TQ
    tk = ATT_TK
    for c in range(seq // tk):
        vt_ref[c, 0:DIFF_DV, :] = v_ref[c * tk:(c + 1) * tk, :].T
        vt_ref[c, DIFF_DV:, :] = jnp.ones((ATT_ONES_ROWS, tk), BF16)

    lam = (jnp.exp(jnp.sum(lq1_ref[...] * lk1_ref[...], axis=-1, keepdims=True))
           - jnp.exp(jnp.sum(lq2_ref[...] * lk2_ref[...], axis=-1, keepdims=True))
           + lambda_init)

    half = tk // 2
    steps = []
    for qi in range(seq // tq):
        for j in range(qi):
            steps.append((qi, j * tk, tk, 0, False, j == 0, False))
        steps.append((qi, qi * tk, half, 0, True, qi == 0, False))
        steps.append((qi, qi * tk + half, half, half, True, False, True))

    def issue_scores(st, slot):
        qi, k0, nk, qlo, _, _, _ = st
        qblk = q_ref[qi * tq + qlo:(qi + 1) * tq, :]
        kblk = k_ref[k0:k0 + nk, :]
        for c in range(2):
            s_ref[slot, c, 0:nk, 0:tq - qlo] = lax.dot_general(
                kblk[:, c * HEAD:(c + 1) * HEAD], qblk[:, c * HEAD:(c + 1) * HEAD],
                (((1,), (1,)), ((), ())), preferred_element_type=F32)

    def consume(st, slot, m):
        qi, k0, nk, qlo, masked, first, _ = st
        nq = tq - qlo
        par = qi % 2
        for c in range(2):
            s = s_ref[slot, c, 0:nk, 0:nq]
            if masked:
                s = s + bias_ref[:, 0:nq]
            mx = jnp.max(s, axis=0, keepdims=True)
            m_old = None if first else m[c][:, qlo:]
            m_new = mx if first else jnp.maximum(m_old, mx)
            p = jnp.exp2((s - m_new).astype(BF16))
            vt = vt_ref[k0 // tk, :, k0 % tk:k0 % tk + nk]
            pv = jnp.dot(vt, p, preferred_element_type=F32)
            if first:
                acc_ref[par, c] = pv
            else:
                acc_ref[par, c, :, qlo:] = (jnp.exp2(m_old - m_new) * acc_ref[par, c, :, qlo:]
                                            + pv)
            m[c] = m_new if qlo == 0 else jnp.concatenate([m[c][:, :qlo], m_new], axis=1)

    sg_scaled = sg_ref[...] * (1.0 - lambda_init)

    def finalize(qi, part):
        dv = DIFF_DV
        par = qi % 2
        ql = slice(part * half, (part + 1) * half)
        w0 = 1.0 / acc_ref[par, 0, dv:dv + 1, ql]
        w1 = lam * (1.0 / acc_ref[par, 1, dv:dv + 1, ql])
        o_t = acc_ref[par, 0, 0:dv, ql] * w0 - acc_ref[par, 1, 0:dv, ql] * w1
        ms = jnp.mean(o_t * o_t, axis=0, keepdims=True)
        o = (o_t * lax.rsqrt(ms + NORM_EPS)).T
        rows = slice(qi * tq + part * half, qi * tq + (part + 1) * half)
        g = g_ref[rows, :].astype(F32)
        o_ref[rows, :] = ((o * sg_scaled) * _silu(g)).astype(o_ref.dtype)

    issue_scores(steps[0], 0)
    m = [None, None]
    pending = []
    for i, st in enumerate(steps):
        if i + 1 < len(steps):
            issue_scores(steps[i + 1], (i + 1) % 2)
        consume(st, i % 2, m)
        if pending:
            finalize(*pending.pop(0))
        if st[6]:
            pending += [(st[0], 0), (st[0], 1)]
    for job in pending:
        finalize(*job)


def _diff_attention(proj, lq1, lk1, lq2, lk2, subln_g, lambda_init, batch, seq):
    m = proj.shape[0]
    tq, tk = ATT_TQ, ATT_TK
    assert tq == tk and seq % tk == 0
    wblk = DIFF_DV
    per_region = DIFF_HEADS
    kk = jnp.arange(tk // 2, dtype=jnp.int32)[:, None]
    qq = jnp.arange(tq, dtype=jnp.int32)[None, :]
    bias = jnp.where(kk <= qq, 0.0, -jnp.inf).astype(F32)

    def blk(region):
        return pl.BlockSpec((seq, wblk), lambda b, h, r=region: (b, r * per_region + h))

    vec = pl.BlockSpec((1, HEAD), lambda b, h: (0, 0))
    return pl.pallas_call(
        functools.partial(_attn_kernel, seq=seq, lambda_init=lambda_init),
        grid=(batch, DIFF_HEADS),
        in_specs=[
            blk(0), blk(1), blk(2), blk(3),
            pl.BlockSpec((tk // 2, tq), lambda b, h: (0, 0)),
            vec, vec, vec, vec,
            pl.BlockSpec((1, wblk), lambda b, h: (0, 0)),
        ],
        out_specs=pl.BlockSpec((seq, wblk), lambda b, h: (b, h)),
        out_shape=jax.ShapeDtypeStruct((m, DIFF_HEADS * wblk), BF16),
        scratch_shapes=[
            pltpu.VMEM((seq // tk, wblk + ATT_ONES_ROWS, tk), BF16),
            pltpu.VMEM((2, 2, wblk + ATT_ONES_ROWS, tq), F32),
            pltpu.VMEM((2, 2, tk, tq), F32),
        ],
        compiler_params=_params(("arbitrary", "arbitrary")),
        name="diff_attn",
    )(proj, proj, proj, proj, bias, lq1.reshape(1, HEAD), lk1.reshape(1, HEAD),
      lq2.reshape(1, HEAD), lk2.reshape(1, HEAD), subln_g.reshape(1, wblk))


def _out_kernel(*refs, n_in):
    acts = refs[:n_in]
    w_ref, h_ref, g_ref, b_ref, o_ref, ob_ref, wb_ref = refs[n_in:]

    @pl.when(pl.program_id(0) == 0)
    def _():
        wb_ref[...] = w_ref[...].astype(BF16)

    for r in range(h_ref.shape[0] // OUT_CHUNK):
        rows = slice(r * OUT_CHUNK, (r + 1) * OUT_CHUNK)
        y = None
        k0 = 0
        for a in acts:
            kw = a.shape[1]
            part = jnp.dot(a[rows, :], wb_ref[k0:k0 + kw, :], preferred_element_type=F32)
            y = part if y is None else y + part
            k0 += kw
        z = DEEPNORM_ALPHA * h_ref[rows, :] + y
        mu = jnp.mean(z, axis=-1, keepdims=True)
        d = z - mu
        var = jnp.mean(d * d, axis=-1, keepdims=True)
        out = d * lax.rsqrt(var + NORM_EPS) * g_ref[...] + b_ref[...]
        o_ref[rows, :] = out
        ob_ref[rows, :] = out.astype(BF16)


def _out_proj_norm(acts, w, layer, h, ln_g, ln_b):
    m, d = h.shape
    tm = OUT_TM
    n_in = len(acts)
    wshape = w.shape[1:]
    assert sum(a.shape[1] for a in acts) == wshape[0]
    in_specs = [pl.BlockSpec((tm, a.shape[1]), lambda i: (i, 0)) for a in acts]
    in_specs += [pl.BlockSpec((None,) + wshape, lambda i: (layer, 0, 0)),
                 pl.BlockSpec((tm, d), lambda i: (i, 0)),
                 pl.BlockSpec((1, d), lambda i: (0, 0)),
                 pl.BlockSpec((1, d), lambda i: (0, 0))]
    return pl.pallas_call(
        functools.partial(_out_kernel, n_in=n_in),
        grid=(m // tm,),
        in_specs=in_specs,
        out_specs=[pl.BlockSpec((tm, d), lambda i: (i, 0)),
                   pl.BlockSpec((tm, d), lambda i: (i, 0))],
        out_shape=[jax.ShapeDtypeStruct((m, d), F32), jax.ShapeDtypeStruct((m, d), BF16)],
        scratch_shapes=[pltpu.VMEM(wshape, BF16)],
        compiler_params=_params(("arbitrary",)),
        name="out_norm",
    )(*acts, w, h, ln_g.reshape(1, d), ln_b.reshape(1, d))


def _rope_tables(seq, scale_first, scale_second):
    half = HEAD // 2
    inv_freq = ROPE_THETA ** (-jnp.arange(half, dtype=F32) / half)
    ang = jnp.arange(seq, dtype=F32)[:, None] * inv_freq[None, :]
    cos = jnp.cos(ang)
    sin = jnp.sin(ang)
    cos_full = jnp.concatenate([cos, cos], axis=-1)
    sin_full = jnp.concatenate([-sin, sin], axis=-1)
    scales = jnp.array([scale_first, scale_second], F32)[:, None, None]
    return cos_full[None] * scales, sin_full[None] * scales


def kernel(x, ev_w_in, ev_conv_w, ev_conv_b, ev_gate_a_w, ev_gate_a_b, ev_gate_x_w, ev_gate_x_b,
           ev_lru_lambda, ev_ret_gn_g, ev_ret_gn_b, ev_w_out, ev_ln_g, ev_ln_b,
           od_w_in, od_lambda_q1, od_lambda_k1, od_lambda_q2, od_lambda_k2, od_subln_g, od_w_out,
           od_ln_g, od_ln_b):
    batch, seq, d = x.shape
    m = batch * seq
    ret_width = RET_HEADS * HEAD
    ev_cos, ev_sin = _rope_tables(seq, 1.0, HEAD ** -0.5)
    od_cos, od_sin = _rope_tables(seq, HEAD ** -0.5 * LOG2E, 1.0)

    h = x.reshape(m, d).astype(F32)
    hb = h
    for l in range(DEPTH):
        j = l // 2
        if l % 2 == 0:
            proj = _project(hb, ev_w_in, j, ev_cos, ev_sin, 2 * ret_width, seq)
            ret = _retention(proj, ev_ret_gn_g[j], ev_ret_gn_b[j], batch, seq)
            lru = _rg_lru(proj, ev_conv_w[j], ev_conv_b[j], ev_gate_a_w[j], ev_gate_a_b[j],
                          ev_gate_x_w[j], ev_gate_x_b[j], ev_lru_lambda[j], batch, seq)
            h, hb = _out_proj_norm([ret, lru], ev_w_out, j, h, ev_ln_g[j], ev_ln_b[j])
        else:
            lambda_init = 0.8 - 0.6 * math.exp(-0.3 * l)
            proj = _project(hb, od_w_in, j, od_cos, od_sin, 2 * 2 * DIFF_HEADS * HEAD, seq)
            mix = _diff_attention(proj, od_lambda_q1[j], od_lambda_k1[j], od_lambda_q2[j],
                                  od_lambda_k2[j], od_subln_g[j], lambda_init, batch, seq)
            h, hb = _out_proj_norm([mix], od_w_out, j, h, od_ln_g[j], od_ln_b[j])
    return h.reshape(batch, seq, d).astype(x.dtype)
```
